```python
import math
import jax, jax.numpy as jnp
from jax import lax
import numpy as np


D_MODEL = 2048
BATCH = 2
SEQ = 16384
DEPTH = 2
DEC_BATCH = 4
DEC_SEQ = 4096
PAST_LEN = 128

GRID_W = 64
RMS_EPS = 1e-6
Q_BLOCK = 128
ADA_CHUNKS = 6
NA_HEADS = 4
NA_DIM = 128
NA_WIN_R = 8
NA_WIN_C = 16
MLA_HEADS = 4
MLA_Q_RANK = 384
MLA_KV_RANK = 256
MLA_NOPE = 128
MLA_ROPE = 64
MLA_V = 128
ROPE_BASE = 10000.0
DIFF_HEADS = 4
DIFF_DIM = 64
DIFF_EPS = 1e-5
SWA_HEADS = 8
SWA_KV_HEADS = 2
SWA_DIM = 64
SWA_WINDOW = 128
MIX_A = NA_HEADS * NA_DIM
MIX_B = MLA_HEADS * MLA_V
MIX_C = DIFF_HEADS * 2 * DIFF_DIM
MIX_D = SWA_HEADS * SWA_DIM
MIX_SPLITS = (MIX_A, MIX_B, MIX_C, MIX_D)
MIX_WIDTH = sum(MIX_SPLITS)
N_BRANCH = 4
IN_SPLITS = (MIX_A, MIX_A, MIX_A,
             MLA_Q_RANK, MLA_KV_RANK, MLA_ROPE,
             MIX_C, MIX_C, MIX_C,
             SWA_HEADS * SWA_DIM, SWA_KV_HEADS * SWA_DIM, SWA_KV_HEADS * SWA_DIM) + (D_MODEL,) * N_BRANCH
IN_WIDTH = sum(IN_SPLITS)
N_EXPERTS = 64
TOP_K = 8
N_GROUPS = 8
TOPK_GROUPS = 4
EXPERT_DIM = 512
SHARED_DIM = 512
ROUTED_SCALE = 2.5
MOE_BLOCK = 256

kernel_name = 'hybrid_gated_encoder'


def rms_norm(x, g, eps=RMS_EPS):
    xf = x.astype(jnp.float32)
    y = xf * lax.rsqrt(jnp.mean(xf * xf, axis=-1, keepdims=True) + eps)
    return (y * g.astype(jnp.float32)).astype(x.dtype)


def modulate(h, shift, scale):
    return h * (1.0 + scale[:, None, :]) + shift[:, None, :]


def rope_tables(n, dim):
    inv = 1.0 / (ROPE_BASE ** (jnp.arange(0, dim, 2, dtype=jnp.float32) / dim))
    ang = jnp.arange(n, dtype=jnp.float32)[:, None] * inv[None, :]
    return jnp.cos(ang), jnp.sin(ang)


def apply_rope(x, cos, sin):
    x1, x2 = jnp.split(x, 2, axis=-1)
    cos = cos.astype(x.dtype)
    sin = sin.astype(x.dtype)
    return jnp.concatenate([x1 * cos - x2 * sin, x2 * cos + x1 * sin], axis=-1)


def alibi_slopes(n):
    return jnp.asarray(2.0 ** (-8.0 * np.arange(1, n + 1) / n), dtype=jnp.float32)


def swiglu(h, wg, wu, wd):
    return (jax.nn.silu(h @ wg) * (h @ wu)) @ wd


def neighbourhood_attention(q, k, v, rpb):
    B, N = q.shape[:2]
    rows = N // GRID_W
    wr = min(NA_WIN_R, rows)
    shp = (B, rows, GRID_W, NA_HEADS, NA_DIM)
    qg, kg, vg = q.reshape(shp), k.reshape(shp), v.reshape(shp)
    cols = np.arange(GRID_W)
    c0 = np.clip(cols - NA_WIN_C // 2, 0, GRID_W - NA_WIN_C)
    col_mask = (cols[None, :] >= c0[:, None]) & (cols[None, :] < c0[:, None] + NA_WIN_C)
    dc_idx = jnp.asarray(np.clip(cols[None, :] - cols[:, None] + NA_WIN_C - 1, 0, 2 * NA_WIN_C - 2), jnp.int32)
    scale = NA_DIM ** -0.5

    def one_row(r):
        r0 = jnp.clip(r - wr // 2, 0, rows - wr)
        qr = lax.dynamic_index_in_dim(qg, r, axis=1, keepdims=False)
        kb = lax.dynamic_slice_in_dim(kg, r0, wr, axis=1)
        vb = lax.dynamic_slice_in_dim(vg, r0, wr, axis=1)
        dr_idx = r0 + jnp.arange(wr) - r + NA_WIN_R - 1
        bias = rpb[:, dr_idx[None, :, None], dc_idx[:, None, :]]
        s = jnp.einsum('bchd,bjkhd->bhcjk', qr, kb).astype(jnp.float32) * scale + bias.astype(jnp.float32)
        s = jnp.where(col_mask[:, None, :], s, -jnp.inf)
        p = jax.nn.softmax(s.reshape(B, NA_HEADS, GRID_W, wr * GRID_W), axis=-1)
        p = p.reshape(B, NA_HEADS, GRID_W, wr, GRID_W).astype(v.dtype)
        return jnp.einsum('bhcjk,bjkhd->bchd', p, vb)

    o = lax.map(one_row, jnp.arange(rows))
    return jnp.swapaxes(o, 0, 1).reshape(B, N, MIX_A)


def mla_attention(q_lat, kv_lat, k_rope, q_norm_g, w_q_up, kv_norm_g, w_kv_up, cos, sin):
    B, N = q_lat.shape[:2]
    nb = N // Q_BLOCK
    q = (rms_norm(q_lat, q_norm_g) @ w_q_up).reshape(B, N, MLA_HEADS, MLA_NOPE + MLA_ROPE)
    q_nope = q[..., :MLA_NOPE]
    q_pe = apply_rope(q[..., MLA_NOPE:], cos[:, None, :], sin[:, None, :])
    kv = (rms_norm(kv_lat, kv_norm_g) @ w_kv_up).reshape(B, N, MLA_HEADS, MLA_NOPE + MLA_V)
    k_nope, v = kv[..., :MLA_NOPE], kv[..., MLA_NOPE:]
    k_pe = apply_rope(k_rope, cos, sin)
    scale = (MLA_NOPE + MLA_ROPE) ** -0.5

    def block(qs):
        qn, qp = qs
        s = (jnp.einsum('bqhd,bshd->bhqs', qn, k_nope) + jnp.einsum('bqhd,bsd->bhqs', qp, k_pe)).astype(jnp.float32) * scale
        p = jax.nn.softmax(s, axis=-1).astype(v.dtype)
        return jnp.einsum('bhqs,bshd->bqhd', p, v)

    def to_blocks(t):
        return jnp.swapaxes(t.reshape((B, nb, Q_BLOCK) + t.shape[2:]), 0, 1)

    o = lax.map(block, (to_blocks(q_nope), to_blocks(q_pe)))
    return jnp.swapaxes(o, 0, 1).reshape(B, N, MIX_B)


def diff_attention(q, k, v, lam, lam_init, subln_g, slopes):
    B, N = q.shape[:2]
    nb = N // Q_BLOCK
    pos = jnp.arange(N)
    scale = DIFF_DIM ** -0.5

    def block(qs):
        qb, qpos = qs
        s = jnp.einsum('bqhmd,bshmd->bmhqs', qb, k).astype(jnp.float32) * scale
        dist = jnp.abs(qpos[:, None] - pos[None, :]).astype(jnp.float32)
        s = s - slopes[:, None, None] * dist
        p = jax.nn.softmax(s, axis=-1)
        a = (p[:, 0] - lam * p[:, 1]).astype(v.dtype)
        return jnp.einsum('bhqs,bshd->bqhd', a, v)

    qb = jnp.swapaxes(q.reshape(B, nb, Q_BLOCK, DIFF_HEADS, 2, DIFF_DIM), 0, 1)
    o = lax.map(block, (qb, pos.reshape(nb, Q_BLOCK)))
    o = jnp.swapaxes(o, 0, 1).reshape(B, N, DIFF_HEADS, 2 * DIFF_DIM)
    o = rms_norm(o, subln_g, DIFF_EPS) * (1.0 - lam_init)
    return o.reshape(B, N, MIX_C)


def swa_attention(q, k, v, sinks, slopes):
    B, N = q.shape[:2]
    nb = N // Q_BLOCK
    G = SWA_HEADS // SWA_KV_HEADS

    def band(t):
        tp = jnp.pad(t, ((0, 0), (Q_BLOCK, Q_BLOCK), (0, 0), (0, 0))).reshape(B, nb + 2, Q_BLOCK, SWA_KV_HEADS, SWA_DIM)
        return jnp.concatenate([tp[:, :-2], tp[:, 1:-1], tp[:, 2:]], axis=2)

    kb, vb = band(k), band(v)
    qb = q.reshape(B, nb, Q_BLOCK, SWA_KV_HEADS, G, SWA_DIM)
    s = jnp.einsum('bnqkgd,bnskd->bnkgqs', qb, kb).astype(jnp.float32) * (SWA_DIM ** -0.5)
    rel = np.arange(3 * Q_BLOCK)[None, :] - Q_BLOCK - np.arange(Q_BLOCK)[:, None]
    kpos = np.arange(nb)[:, None] * Q_BLOCK - Q_BLOCK + np.arange(3 * Q_BLOCK)[None, :]
    valid = (np.abs(rel) <= SWA_WINDOW)[None] & ((kpos >= 0) & (kpos < N))[:, None, :]
    sl = slopes.reshape(SWA_KV_HEADS, G)[:, :, None, None]
    s = s - sl * jnp.asarray(np.abs(rel), jnp.float32)
    s = jnp.where(jnp.asarray(valid)[None, :, None, None], s, -jnp.inf)
    sink = sinks.astype(jnp.float32).reshape(SWA_KV_HEADS, G)[:, :, None, None]
    m = jnp.maximum(s.max(axis=-1, keepdims=True), sink)
    e = jnp.exp(s - m)
    p = e / (e.sum(axis=-1, keepdims=True) + jnp.exp(sink - m))
    o = jnp.einsum('bnkgqs,bnskd->bnqkgd', p.astype(v.dtype), vb)
    return o.reshape(B, N, MIX_D)


def mixer(h, p, l, cos, sin, diff_slopes, swa_slopes):
    B, N, _ = h.shape
    z = h @ p['w_in'][l]
    parts = jnp.split(z, np.cumsum(IN_SPLITS)[:-1].tolist(), axis=-1)
    qa, ka, va, q_lat, kv_lat, k_rope, qc, kc, vc, qd, kd, vd = parts[:12]
    gates = parts[12:]
    na_shape = (B, N, NA_HEADS, NA_DIM)
    o_a = neighbourhood_attention(qa.reshape(na_shape), ka.reshape(na_shape), va.reshape(na_shape), p['na_rpb'][l])
    o_b = mla_attention(q_lat, kv_lat, k_rope, p['mla_q_norm_g'][l], p['mla_w_q_up'][l],
                        p['mla_kv_norm_g'][l], p['mla_w_kv_up'][l], cos, sin)
    lam_init = 0.8 - 0.6 * math.exp(-0.3 * l)
    f32 = jnp.float32
    lam = (jnp.exp(jnp.sum(p['diff_lambda_q1'][l].astype(f32) * p['diff_lambda_k1'][l].astype(f32)))
           - jnp.exp(jnp.sum(p['diff_lambda_q2'][l].astype(f32) * p['diff_lambda_k2'][l].astype(f32))) + lam_init)
    d_shape = (B, N, DIFF_HEADS, 2, DIFF_DIM)
    o_c = diff_attention(qc.reshape(d_shape), kc.reshape(d_shape), vc.reshape(B, N, DIFF_HEADS, 2 * DIFF_DIM),
                         lam, lam_init, p['diff_subln_g'][l], diff_slopes)
    o_d = swa_attention(qd.reshape(B, N, SWA_HEADS, SWA_DIM), kd.reshape(B, N, SWA_KV_HEADS, SWA_DIM),
                        vd.reshape(B, N, SWA_KV_HEADS, SWA_DIM), p['swa_sinks'][l], swa_slopes)
    w_parts = jnp.split(p['w_branch'][l], np.cumsum(MIX_SPLITS)[:-1].tolist(), axis=0)
    merged = jnp.zeros_like(h)
    for o, wb, g in zip((o_a, o_b, o_c, o_d), w_parts, gates):
        merged = merged + jax.nn.sigmoid(g) * (o @ wb)
    return merged @ p['w_o'][l]


def routed_experts(h, eidx, w, w_gate, w_up, w_down):
    T, D = h.shape
    TK = T * TOP_K
    e_flat = eidx.reshape(TK)
    tok_flat = jnp.repeat(jnp.arange(T, dtype=jnp.int32), TOP_K)
    w_flat = w.reshape(TK)
    order = jnp.argsort(e_flat)
    e_sorted = e_flat[order]
    counts = jnp.bincount(e_flat, length=N_EXPERTS)
    padded = (counts + MOE_BLOCK - 1) // MOE_BLOCK * MOE_BLOCK
    start = jnp.cumsum(counts) - counts
    p_end = jnp.cumsum(padded)
    p_start = p_end - padded
    dest = p_start[e_sorted] + jnp.arange(TK) - start[e_sorted]
    n_blocks = -(-TK // MOE_BLOCK) + N_EXPERTS
    P = n_blocks * MOE_BLOCK
    tok_buf = jnp.full((P,), T, jnp.int32).at[dest].set(tok_flat[order])
    w_buf = jnp.zeros((P,), h.dtype).at[dest].set(w_flat[order])
    blk_e = jnp.minimum(jnp.searchsorted(p_end, jnp.arange(n_blocks) * MOE_BLOCK, side='right'), N_EXPERTS - 1)
    h_pad = jnp.concatenate([h, jnp.zeros((1, D), h.dtype)], axis=0)

    def step(acc, blk):
        tok, wt, e = blk
        y = swiglu(h_pad[tok], w_gate[e], w_up[e], w_down[e]) * wt[:, None]
        return acc.at[tok].add(y.astype(acc.dtype)), None

    acc, _ = lax.scan(step, jnp.zeros((T + 1, D), h.dtype),
                      (tok_buf.reshape(n_blocks, MOE_BLOCK), w_buf.reshape(n_blocks, MOE_BLOCK), blk_e))
    return acc[:T]


def moe(h, p, l):
    T = h.shape[0]
    s = jax.nn.sigmoid((h @ p['w_router'][l]).astype(jnp.float32))
    sel = s + p['b_router'][l].astype(jnp.float32)
    grp_score = lax.top_k(sel.reshape(T, N_GROUPS, N_EXPERTS // N_GROUPS), 2)[0].sum(-1)
    _, gidx = lax.top_k(grp_score, TOPK_GROUPS)
    gmask = jnp.any(gidx[:, :, None] == jnp.arange(N_GROUPS)[None, None, :], axis=1)
    emask = jnp.repeat(gmask, N_EXPERTS // N_GROUPS, axis=1)
    _, eidx = lax.top_k(jnp.where(emask, sel, -jnp.inf), TOP_K)
    w = jnp.take_along_axis(s, eidx, axis=1)
    w = w / w.sum(-1, keepdims=True) * ROUTED_SCALE
    routed = routed_experts(h, eidx, w.astype(h.dtype), p['w_exp_gate'][l], p['w_exp_up'][l], p['w_exp_down'][l])
    return routed + swiglu(h, p['w_sh_gate'][l], p['w_sh_up'][l], p['w_sh_down'][l])


def trunk(x, c, p):
    B, N, _ = x.shape
    cos, sin = rope_tables(N, MLA_ROPE)
    diff_slopes = alibi_slopes(DIFF_HEADS)
    swa_slopes = alibi_slopes(SWA_HEADS)
    for l in range(DEPTH):
        mod = jax.nn.silu(c) @ p['w_ada'][l] + p['b_ada'][l]
        sh1, sc1, g1, sh2, sc2, g2 = jnp.split(mod, ADA_CHUNKS, axis=-1)
        h = modulate(rms_norm(x, p['norm1_g'][l]), sh1, sc1)
        x = x + g1[:, None, :] * mixer(h, p, l, cos, sin, diff_slopes, swa_slopes)
        h = modulate(rms_norm(x, p['norm2_g'][l]), sh2, sc2)
        x = x + g2[:, None, :] * moe(h.reshape(B * N, D_MODEL), p, l).reshape(B, N, D_MODEL)
    return rms_norm(x, p['final_g'])


def setup_inputs(seed: int = 0) -> dict:
    key = jax.random.key(seed)
    ks = iter(jax.random.split(key, 32))

    def nrm(shape, scale):
        return jax.random.normal(next(ks), shape, jnp.float32) * scale

    def gain(shape):
        return 1.0 + nrm(shape, 0.02)

    L, D = DEPTH, D_MODEL
    return {
        'x_prompt': nrm((BATCH, SEQ, D), 1.0),
        'x_sample': nrm((DEC_BATCH, DEC_SEQ, D), 1.0),
        'c_prompt': nrm((BATCH, D), 1.0),
        'c_sample': nrm((DEC_BATCH, D), 1.0),
        'norm1_g': gain((L, D)),
        'w_ada': nrm((L, D, ADA_CHUNKS * D), 0.5 * D ** -0.5),
        'b_ada': nrm((L, ADA_CHUNKS * D), 0.02),
        'w_in': nrm((L, D, IN_WIDTH), D ** -0.5),
        'na_rpb': nrm((L, NA_HEADS, 2 * NA_WIN_R - 1, 2 * NA_WIN_C - 1), 0.2),
        'mla_q_norm_g': gain((L, MLA_Q_RANK)),
        'mla_w_q_up': nrm((L, MLA_Q_RANK, MLA_HEADS * (MLA_NOPE + MLA_ROPE)), MLA_Q_RANK ** -0.5),
        'mla_kv_norm_g': gain((L, MLA_KV_RANK)),
        'mla_w_kv_up': nrm((L, MLA_KV_RANK, MLA_HEADS * (MLA_NOPE + MLA_V)), MLA_KV_RANK ** -0.5),
        'diff_lambda_q1': nrm((L, DIFF_DIM), 0.1),
        'diff_lambda_k1': nrm((L, DIFF_DIM), 0.1),
        'diff_lambda_q2': nrm((L, DIFF_DIM), 0.1),
        'diff_lambda_k2': nrm((L, DIFF_DIM), 0.1),
        'diff_subln_g': gain((L, 2 * DIFF_DIM)),
        'swa_sinks': nrm((L, SWA_HEADS), 0.5),
        'w_branch': nrm((L, MIX_WIDTH, D), (MIX_WIDTH // N_BRANCH) ** -0.5),
        'w_o': nrm((L, D, D), D ** -0.5),
        'norm2_g': gain((L, D)),
        'w_router': nrm((L, D, N_EXPERTS), D ** -0.5),
        'b_router': nrm((L, N_EXPERTS), 0.01),
        'w_exp_gate': nrm((L, N_EXPERTS, D, EXPERT_DIM), D ** -0.5),
        'w_exp_up': nrm((L, N_EXPERTS, D, EXPERT_DIM), D ** -0.5),
        'w_exp_down': nrm((L, N_EXPERTS, EXPERT_DIM, D), EXPERT_DIM ** -0.5),
        'w_sh_gate': nrm((L, D, SHARED_DIM), D ** -0.5),
        'w_sh_up': nrm((L, D, SHARED_DIM), D ** -0.5),
        'w_sh_down': nrm((L, SHARED_DIM, D), SHARED_DIM ** -0.5),
        'final_g': gain((D,)),
    }


def reference(x_prompt, x_sample, c_prompt, c_sample, norm1_g, w_ada, b_ada, w_in, na_rpb,
              mla_q_norm_g, mla_w_q_up, mla_kv_norm_g, mla_w_kv_up,
              diff_lambda_q1, diff_lambda_k1, diff_lambda_q2, diff_lambda_k2, diff_subln_g,
              swa_sinks, w_branch, w_o, norm2_g, w_router, b_router,
              w_exp_gate, w_exp_up, w_exp_down, w_sh_gate, w_sh_up, w_sh_down, final_g):
    p = dict(norm1_g=norm1_g, w_ada=w_ada, b_ada=b_ada, w_in=w_in, na_rpb=na_rpb,
             mla_q_norm_g=mla_q_norm_g, mla_w_q_up=mla_w_q_up, mla_kv_norm_g=mla_kv_norm_g, mla_w_kv_up=mla_w_kv_up,
             diff_lambda_q1=diff_lambda_q1, diff_lambda_k1=diff_lambda_k1,
             diff_lambda_q2=diff_lambda_q2, diff_lambda_k2=diff_lambda_k2, diff_subln_g=diff_subln_g,
             swa_sinks=swa_sinks, w_branch=w_branch, w_o=w_o, norm2_g=norm2_g,
             w_router=w_router, b_router=b_router, w_exp_gate=w_exp_gate, w_exp_up=w_exp_up,
             w_exp_down=w_exp_down, w_sh_gate=w_sh_gate, w_sh_up=w_sh_up, w_sh_down=w_sh_down,
             final_g=final_g)
    y_prompt = trunk(x_prompt, c_prompt, p)
    y_sample = trunk(x_sample, c_sample, p)
    return (y_prompt, y_sample)
```

```python
import functools
import math

import numpy as np
import jax
import jax.numpy as jnp
from jax import lax
from jax.experimental import pallas as pl
from jax.experimental.pallas import tpu as pltpu

BF = jnp.bfloat16
F32 = jnp.float32
LOG2E = 1.4426950408889634
NEG_INF = float("-inf")

VMEM_LIMIT_BYTES = 56 * 1024 * 1024

D_MODEL = 2048
GRID_W = 64
RMS_EPS = 1e-6
ADA_CHUNKS = 6
NA_HEADS, NA_DIM, NA_WIN_R, NA_WIN_C = 4, 128, 8, 16
MLA_HEADS, MLA_Q_RANK, MLA_KV_RANK, MLA_NOPE, MLA_ROPE, MLA_V = 4, 384, 256, 128, 64, 128
ROPE_BASE = 10000.0
DIFF_HEADS, DIFF_DIM, DIFF_EPS = 4, 64, 1e-5
SWA_HEADS, SWA_KV_HEADS, SWA_DIM, SWA_WINDOW = 8, 2, 64, 128
SWA_BLOCK = 128
MIX = 512
N_BRANCH = 4
N_EXPERTS, TOP_K, N_GROUPS, TOPK_GROUPS = 64, 8, 8, 4
GROUP_SIZE = N_EXPERTS // N_GROUPS
EXPERT_DIM, SHARED_DIM = 512, 512
ROUTED_SCALE = 2.5
MOE_BLOCK = 256

_O_QA, _O_KA, _O_VA = 0, 512, 1024
_O_QLAT, _O_KVLAT, _O_KROPE = 1536, 1920, 2176
_O_QC, _O_KC, _O_VC = 2240, 2752, 3264
_O_QD, _O_KD, _O_VD = 3776, 4288, 4416
_O_GATE = 4544
Z_QA, Z_KA, Z_VA = 0, 512, 1024
Z_QC, Z_KC, Z_VC = 1536, 2048, 2560
Z_QD = 3072
Z_KVD = 3584
Z_MLA = 3840
Z_MLA_W = 768
Z_GATE = 4608
Z_WIDTH = Z_GATE + N_BRANCH * D_MODEL


def _in_proj_columns():
    src = np.zeros(Z_WIDTH, np.int32)
    mul = np.zeros(Z_WIDTH, np.float32)

    def put(dst, start, n, scale=1.0):
        src[dst:dst + n] = np.arange(start, start + n)
        mul[dst:dst + n] = scale

    put(Z_QA, _O_QA, 512, NA_DIM ** -0.5 * LOG2E)
    put(Z_KA, _O_KA, 512)
    put(Z_VA, _O_VA, 512)
    put(Z_QC, _O_QC, 512, DIFF_DIM ** -0.5 * LOG2E)
    put(Z_KC, _O_KC, 512)
    put(Z_VC, _O_VC, 512)
    put(Z_QD, _O_QD, 512, SWA_DIM ** -0.5 * LOG2E)
    put(Z_KVD, _O_KD, 128)
    put(Z_KVD + 128, _O_VD, 128)
    put(Z_MLA, _O_QLAT, MLA_Q_RANK)
    put(Z_MLA + 384, _O_KVLAT, MLA_KV_RANK)
    put(Z_MLA + 640, _O_KROPE, MLA_ROPE)
    half = MLA_ROPE // 2
    put(Z_MLA + 704, _O_KROPE + half, half, -1.0)
    put(Z_MLA + 704 + half, _O_KROPE, half, 1.0)
    put(Z_GATE, _O_GATE, N_BRANCH * D_MODEL)
    return src, mul


_IN_SRC, _IN_MUL = _in_proj_columns()


def _params(*sem):
    return pltpu.CompilerParams(dimension_semantics=sem, vmem_limit_bytes=VMEM_LIMIT_BYTES)


def _dot(a, b):
    return jnp.dot(a, b, preferred_element_type=F32)


def _dot_nt(a, b, **kw):
    return lax.dot_general(a, b, (((1,), (1,)), ((), ())), preferred_element_type=F32, **kw)


def _sigmoid(x):
    return 1.0 / (1.0 + jnp.exp(-x))


def _silu(x):
    return x * _sigmoid(x)


def _rms(x, g, eps):
    return x * lax.rsqrt(jnp.mean(x * x, axis=-1, keepdims=True) + eps) * g


def _ada_kernel(c_ref, w_ref, b_ref, o_ref):
    a = _silu(c_ref[...]).astype(BF)
    o_ref[0] = _dot(a, w_ref[0].astype(BF)) + b_ref[0]


def _ada(c_all, w_ada, b_ada):
    L, D, W = w_ada.shape
    tn = 768
    return pl.pallas_call(
        _ada_kernel,
        grid=(L, W // tn),
        in_specs=[pl.BlockSpec((8, D), lambda l, j: (0, 0)),
                  pl.BlockSpec((1, D, tn), lambda l, j: (l, 0, j)),
                  pl.BlockSpec((1, 1, tn), lambda l, j: (l, 0, j))],
        out_specs=pl.BlockSpec((1, 8, tn), lambda l, j: (l, 0, j)),
        out_shape=jax.ShapeDtypeStruct((L, 8, W), F32),
        compiler_params=_params("parallel", "parallel"),
        name="ada",
    )(c_all, w_ada, b_ada.reshape(L, 1, W))


def _in_proj_kernel(x_ref, g_ref, sh_ref, sc_ref, w_ref, z_ref, h_ref):
    @pl.when(pl.program_id(1) == 0)
    def _():
        h = _rms(x_ref[...], g_ref[...], RMS_EPS) * (1.0 + sc_ref[0]) + sh_ref[0]
        h_ref[...] = h.astype(BF)

    z_ref[...] = _dot(h_ref[...], w_ref[...]).astype(BF)


def _in_proj(x, g, sh, sc, w, n_seq):
    T, D = x.shape
    tm, tn = 512, 1280
    tpb = n_seq // tm
    return pl.pallas_call(
        _in_proj_kernel,
        grid=(T // tm, Z_WIDTH // tn),
        in_specs=[pl.BlockSpec((tm, D), lambda i, j: (i, 0)),
                  pl.BlockSpec((1, D), lambda i, j: (0, 0)),
                  pl.BlockSpec((1, 1, D), lambda i, j: (i // tpb, 0, 0)),
                  pl.BlockSpec((1, 1, D), lambda i, j: (i // tpb, 0, 0)),
                  pl.BlockSpec((D, tn), lambda i, j: (0, j))],
        out_specs=pl.BlockSpec((tm, tn), lambda i, j: (i, j)),
        out_shape=jax.ShapeDtypeStruct((T, Z_WIDTH), BF),
        scratch_shapes=[pltpu.VMEM((tm, D), BF)],
        compiler_params=_params("parallel", "arbitrary"),
        name="in_proj",
    )(x, g, sh, sc, w)


NA_ROWS_PER_STEP = 8
NA_TOK = NA_ROWS_PER_STEP * GRID_W


def _na_bias_table(rpb):
    cols = np.arange(GRID_W)
    c0 = np.clip(cols - NA_WIN_C // 2, 0, GRID_W - NA_WIN_C)
    col_mask = (cols[None, :] >= c0[:, None]) & (cols[None, :] < c0[:, None] + NA_WIN_C)
    dc = np.clip(cols[None, :] - cols[:, None] + NA_WIN_C - 1, 0, 2 * NA_WIN_C - 2)
    dr = np.arange(NA_WIN_R)[:, None] + np.arange(NA_WIN_R)[None, :]
    tab = rpb.astype(F32)[:, dr[:, None, :, None], dc[None, :, None, :]]
    tab = jnp.where(col_mask[None, None, :, None, :], tab * LOG2E, NEG_INF)
    return jnp.transpose(tab, (1, 0, 2, 3, 4)).reshape(NA_WIN_R, NA_HEADS, GRID_W, NA_WIN_R * GRID_W)


def _na_kernel(rows, q_ref, kp_ref, kc_ref, kn_ref, vp_ref, vc_ref, vn_ref, bias_ref, o_ref, kbuf, vbuf):
    i = pl.program_id(1)
    kbuf[0:NA_TOK] = kp_ref[...]
    kbuf[NA_TOK:2 * NA_TOK] = kc_ref[...]
    kbuf[2 * NA_TOK:3 * NA_TOK] = kn_ref[...]
    vbuf[0:NA_TOK] = vp_ref[...]
    vbuf[NA_TOK:2 * NA_TOK] = vc_ref[...]
    vbuf[2 * NA_TOK:3 * NA_TOK] = vn_ref[...]
    win = NA_WIN_R * GRID_W
    for rl in range(NA_ROWS_PER_STEP):
        r = i * NA_ROWS_PER_STEP + rl
        r0 = jnp.clip(r - NA_WIN_R // 2, 0, rows - NA_WIN_R)
        variant = r0 - r + NA_WIN_R - 1
        off = pl.multiple_of((r0 - (i - 1) * NA_ROWS_PER_STEP) * GRID_W, GRID_W)
        kw = kbuf[pl.ds(off, win), :]
        vw = vbuf[pl.ds(off, win), :]
        q = q_ref[rl * GRID_W:(rl + 1) * GRID_W, :]
        for h in range(NA_HEADS):
            hs = slice(h * NA_DIM, (h + 1) * NA_DIM)
            s = _dot_nt(q[:, hs], kw[:, hs]) + bias_ref[variant, h]
            m = jnp.max(s, axis=1, keepdims=True)
            p = jnp.exp2(s - m)
            l = jnp.sum(p, axis=1, keepdims=True)
            o = _dot(p.astype(BF), vw[:, hs]) / l
            o_ref[rl * GRID_W:(rl + 1) * GRID_W, hs] = o.astype(BF)


def _na_attention(z, bias, batch, n_seq):
    T = z.shape[0]
    rows = n_seq // GRID_W
    nblk = rows // NA_ROWS_PER_STEP
    assert rows % NA_ROWS_PER_STEP == 0 and rows >= 2 * NA_ROWS_PER_STEP

    def cur(c):
        return pl.BlockSpec((NA_TOK, MIX), lambda b, i: (b * nblk + i, c))

    def prev(c):
        return pl.BlockSpec((NA_TOK, MIX), lambda b, i: (b * nblk + jnp.maximum(i - 1, 0), c))

    def nxt(c):
        return pl.BlockSpec((NA_TOK, MIX), lambda b, i: (b * nblk + jnp.minimum(i + 1, nblk - 1), c))

    ck, cv = Z_KA // MIX, Z_VA // MIX
    return pl.pallas_call(
        functools.partial(_na_kernel, rows),
        grid=(batch, nblk),
        in_specs=[cur(Z_QA // MIX), prev(ck), cur(ck), nxt(ck), prev(cv), cur(cv), nxt(cv),
                  pl.BlockSpec(bias.shape, lambda b, i: (0, 0, 0, 0))],
        out_specs=pl.BlockSpec((NA_TOK, MIX), lambda b, i: (b * nblk + i, 0)),
        out_shape=jax.ShapeDtypeStruct((T, MIX), BF),
        scratch_shapes=[pltpu.VMEM((3 * NA_TOK, MIX), BF), pltpu.VMEM((3 * NA_TOK, MIX), BF)],
        compiler_params=_params("parallel", "parallel"),
        name="na_attention",
    )(z, z, z, z, z, z, z, bias)


MLA_QK = 256
MLA_Q_SCALE = (MLA_NOPE + MLA_ROPE) ** -0.5 * LOG2E


def _mla_weights(w_q_up, w_kv_up):
    half = MLA_ROPE // 2
    wq = w_q_up.reshape(MLA_Q_RANK, MLA_HEADS, MLA_NOPE + MLA_ROPE)
    pe = wq[:, :, MLA_NOPE:]
    rot = jnp.concatenate([-pe[:, :, half:], pe[:, :, :half]], axis=-1)
    wq = jnp.concatenate([wq, rot], axis=-1).reshape(MLA_Q_RANK, MLA_HEADS * MLA_QK)
    wkv = w_kv_up.reshape(MLA_KV_RANK, MLA_HEADS, MLA_NOPE + MLA_V)
    wk = wkv[:, :, :MLA_NOPE].reshape(MLA_KV_RANK, MLA_HEADS * MLA_NOPE)
    wv = wkv[:, :, MLA_NOPE:].reshape(MLA_KV_RANK, MLA_HEADS * MLA_V)
    return wq.astype(BF), wk.astype(BF), wv.astype(BF)


def _rope_table(n):
    inv = 1.0 / (ROPE_BASE ** (jnp.arange(0, MLA_ROPE, 2, dtype=F32) / MLA_ROPE))
    ang = jnp.arange(n, dtype=F32)[:, None] * inv[None, :]
    c, s = jnp.cos(ang), jnp.sin(ang)
    return jnp.concatenate([c, c, s, s], axis=-1)


def _mla_prep_kernel(z_ref, cs_ref, gq_ref, gkv_ref, wq_ref, wk_ref, wv_ref, q_ref, k_ref, v_ref):
    z = z_ref[...].astype(F32)
    qn = _rms(z[:, :MLA_Q_RANK], gq_ref[...], RMS_EPS).astype(BF)
    kvn = _rms(z[:, MLA_Q_RANK:MLA_Q_RANK + MLA_KV_RANK], gkv_ref[...], RMS_EPS).astype(BF)
    q = _dot(qn, wq_ref[...])
    kn = _dot(kvn, wk_ref[...])
    v_ref[...] = _dot(kvn, wv_ref[...]).astype(BF)
    cs = cs_ref[...]
    keep = lax.broadcasted_iota(jnp.int32, cs.shape, 1) < MLA_ROPE

    def rope(g):
        t = g * cs
        return jnp.where(keep, t + pltpu.roll(t, MLA_ROPE, 1), 0.0)

    k_pe = rope(z[:, 640:768]).astype(BF)
    for h in range(MLA_HEADS):
        lo, mid, hi = h * MLA_QK, h * MLA_QK + MLA_NOPE, (h + 1) * MLA_QK
        q_ref[:, lo:mid] = (q[:, lo:mid] * MLA_Q_SCALE).astype(BF)
        q_ref[:, mid:hi] = (rope(q[:, mid:hi]) * MLA_Q_SCALE).astype(BF)
        k_ref[:, lo:mid] = kn[:, h * MLA_NOPE:(h + 1) * MLA_NOPE].astype(BF)
        k_ref[:, mid:hi] = k_pe


def _mla_prep(z, cs, gq, gkv, wq, wk, wv, n_seq):
    T = z.shape[0]
    tm = 512
    tpb = n_seq // tm

    def full(a):
        return pl.BlockSpec(a.shape, lambda i: (0, 0))

    return pl.pallas_call(
        _mla_prep_kernel,
        grid=(T // tm,),
        in_specs=[pl.BlockSpec((tm, Z_MLA_W), lambda i: (i, Z_MLA // Z_MLA_W)),
                  pl.BlockSpec((tm, 128), lambda i: (i % tpb, 0)),
                  full(gq), full(gkv), full(wq), full(wk), full(wv)],
        out_specs=[pl.BlockSpec((tm, MLA_HEADS * MLA_QK), lambda i: (i, 0)),
                   pl.BlockSpec((tm, MLA_HEADS * MLA_QK), lambda i: (i, 0)),
                   pl.BlockSpec((tm, MLA_HEADS * MLA_V), lambda i: (i, 0))],
        out_shape=[jax.ShapeDtypeStruct((T, MLA_HEADS * MLA_QK), BF),
                   jax.ShapeDtypeStruct((T, MLA_HEADS * MLA_QK), BF),
                   jax.ShapeDtypeStruct((T, MLA_HEADS * MLA_V), BF)],
        compiler_params=_params("parallel"),
        name="mla_prep",
    )(z, cs, gq, gkv, wq, wk, wv)


def _flash_init(m_ref, l_ref, acc_ref):
    m_ref[...] = jnp.full(m_ref.shape, NEG_INF, F32)
    l_ref[...] = jnp.zeros(l_ref.shape, F32)
    acc_ref[...] = jnp.zeros(acc_ref.shape, F32)


def _flash_step(s, v, m_ref, l_ref, acc_ref):
    m_prev = m_ref[...]
    m_new = jnp.maximum(m_prev, jnp.max(s, axis=1, keepdims=True))
    alpha = jnp.exp2(m_prev - m_new)
    p = jnp.exp2(s - m_new)
    l_ref[...] = alpha * l_ref[...] + jnp.sum(p, axis=1, keepdims=True)
    acc_ref[...] = alpha * acc_ref[...] + _dot(p.astype(BF), v)
    m_ref[...] = m_new


def _mla_flash_kernel(q_ref, k_ref, v_ref, o_ref, m_ref, l_ref, acc_ref):
    j = pl.program_id(3)

    @pl.when(j == 0)
    def _():
        _flash_init(m_ref, l_ref, acc_ref)

    _flash_step(_dot_nt(q_ref[...], k_ref[...]), v_ref[...], m_ref, l_ref, acc_ref)

    @pl.when(j == pl.num_programs(3) - 1)
    def _():
        o_ref[...] = (acc_ref[...] / l_ref[...]).astype(BF)


def _mla_flash(q, k, v, batch, n_seq):
    T = q.shape[0]
    tq, tk = 512, 512
    nq, nk = n_seq // tq, n_seq // tk
    return pl.pallas_call(
        _mla_flash_kernel,
        grid=(batch, MLA_HEADS, nq, nk),
        in_specs=[pl.BlockSpec((tq, MLA_QK), lambda b, h, i, j: (b * nq + i, h)),
                  pl.BlockSpec((tk, MLA_QK), lambda b, h, i, j: (b * nk + j, h)),
                  pl.BlockSpec((tk, MLA_V), lambda b, h, i, j: (b * nk + j, h))],
        out_specs=pl.BlockSpec((tq, MLA_V), lambda b, h, i, j: (b * nq + i, h)),
        out_shape=jax.ShapeDtypeStruct((T, MLA_HEADS * MLA_V), BF),
        scratch_shapes=[pltpu.VMEM((tq, 1), F32), pltpu.VMEM((tq, 1), F32), pltpu.VMEM((tq, MLA_V), F32)],
        compiler_params=_params("parallel", "parallel", "parallel", "arbitrary"),
        name="mla_flash",
    )(q, k, v)


DIFF_TQ, DIFF_TK = 512, 512


def _diff_flash_kernel(par_ref, q_ref, k_ref, v_ref, d_ref, g_ref, o_ref,
                       q1_ref, q2_ref, m1_ref, l1_ref, a1_ref, m2_ref, l2_ref, a2_ref):
    h, i, j = pl.program_id(1), pl.program_id(2), pl.program_id(3)

    @pl.when(j == 0)
    def _():
        q = q_ref[...]
        first = lax.broadcasted_iota(jnp.int32, q.shape, 1) < DIFF_DIM
        zero = jnp.zeros_like(q)
        q1_ref[...] = jnp.where(first, q, zero)
        q2_ref[...] = jnp.where(first, zero, q)
        _flash_init(m1_ref, l1_ref, a1_ref)
        _flash_init(m2_ref, l2_ref, a2_ref)

    k = k_ref[...]
    v = v_ref[...]
    offset = (i * DIFF_TQ - j * DIFF_TK).astype(F32)
    bias = par_ref[h] * jnp.abs(d_ref[...] + offset)
    _flash_step(_dot_nt(q1_ref[...], k) - bias, v, m1_ref, l1_ref, a1_ref)
    _flash_step(_dot_nt(q2_ref[...], k) - bias, v, m2_ref, l2_ref, a2_ref)

    @pl.when(j == pl.num_programs(3) - 1)
    def _():
        lam, out_scale = par_ref[DIFF_HEADS], par_ref[DIFF_HEADS + 1]
        o = a1_ref[...] / l1_ref[...] - lam * (a2_ref[...] / l2_ref[...])
        o_ref[...] = (_rms(o, g_ref[...], DIFF_EPS) * out_scale).astype(BF)


def _diff_flash(z, par, subln_g, batch, n_seq):
    T = z.shape[0]
    tq, tk = DIFF_TQ, DIFF_TK
    nq, nk = n_seq // tq, n_seq // tk
    w = 2 * DIFF_DIM
    delta = (np.arange(tq)[:, None] - np.arange(tk)[None, :]).astype(np.float32)
    stat = lambda: pltpu.VMEM((tq, 1), F32)
    acc = lambda: pltpu.VMEM((tq, w), F32)
    return pl.pallas_call(
        _diff_flash_kernel,
        grid=(batch, DIFF_HEADS, nq, nk),
        in_specs=[pl.BlockSpec(memory_space=pltpu.SMEM),
                  pl.BlockSpec((tq, w), lambda b, h, i, j: (b * nq + i, Z_QC // w + h)),
                  pl.BlockSpec((tk, w), lambda b, h, i, j: (b * nk + j, Z_KC // w + h)),
                  pl.BlockSpec((tk, w), lambda b, h, i, j: (b * nk + j, Z_VC // w + h)),
                  pl.BlockSpec((tq, tk), lambda b, h, i, j: (0, 0)),
                  pl.BlockSpec((1, w), lambda b, h, i, j: (0, 0))],
        out_specs=pl.BlockSpec((tq, w), lambda b, h, i, j: (b * nq + i, h)),
        out_shape=jax.ShapeDtypeStruct((T, DIFF_HEADS * w), BF),
        scratch_shapes=[pltpu.VMEM((tq, w), BF), pltpu.VMEM((tq, w), BF),
                        stat(), stat(), acc(), stat(), stat(), acc()],
        compiler_params=_params("parallel", "parallel", "parallel", "arbitrary"),
        name="diff_flash",
    )(par, z, z, z, jnp.asarray(delta), subln_g)


def _alibi_slopes(n):
    return 2.0 ** (-8.0 * np.arange(1, n + 1) / n)


def _swa_bias_table():
    rel = np.arange(3 * SWA_BLOCK)[None, :] - SWA_BLOCK - np.arange(SWA_BLOCK)[:, None]
    dist = np.abs(rel).astype(np.float64)
    tab = -_alibi_slopes(SWA_HEADS)[:, None, None] * dist[None] * LOG2E
    tab = np.where((dist <= SWA_WINDOW)[None], tab, -np.inf)
    return tab.astype(np.float32)


def _swa_kernel(sink_ref, q_ref, kvp_ref, kvc_ref, kvn_ref, bias_ref, o_ref):
    i = pl.program_id(1)
    last = pl.num_programs(1) - 1
    kv = jnp.concatenate([kvp_ref[...], kvc_ref[...], kvn_ref[...]], axis=0)
    col = lax.broadcasted_iota(jnp.int32, (SWA_BLOCK, 3 * SWA_BLOCK), 1)
    outside = ((col < SWA_BLOCK) & (i == 0)) | ((col >= 2 * SWA_BLOCK) & (i == last))
    group = SWA_HEADS // SWA_KV_HEADS
    for kvh in range(SWA_KV_HEADS):
        k = kv[:, kvh * SWA_DIM:(kvh + 1) * SWA_DIM]
        v = kv[:, (SWA_KV_HEADS + kvh) * SWA_DIM:(SWA_KV_HEADS + kvh + 1) * SWA_DIM]
        for g in range(group):
            hh = kvh * group + g
            hs = slice(hh * SWA_DIM, (hh + 1) * SWA_DIM)
            s = _dot_nt(q_ref[:, hs], k) + bias_ref[hh]
            s = jnp.where(outside, NEG_INF, s)
            sink = sink_ref[hh]
            m = jnp.maximum(jnp.max(s, axis=1, keepdims=True), sink)
            e = jnp.exp2(s - m)
            denom = jnp.sum(e, axis=1, keepdims=True) + jnp.exp2(sink - m)
            o_ref[:, hs] = (_dot(e.astype(BF), v) / denom).astype(BF)


def _swa_attention(z, sinks2, batch, n_seq):
    T = z.shape[0]
    nb = n_seq // SWA_BLOCK
    kvw = 2 * SWA_KV_HEADS * SWA_DIM
    ckv = Z_KVD // kvw
    bias = jnp.asarray(_swa_bias_table())
    return pl.pallas_call(
        _swa_kernel,
        grid=(batch, nb),
        in_specs=[pl.BlockSpec(memory_space=pltpu.SMEM),
                  pl.BlockSpec((SWA_BLOCK, MIX), lambda b, i: (b * nb + i, Z_QD // MIX)),
                  pl.BlockSpec((SWA_BLOCK, kvw), lambda b, i: (b * nb + jnp.maximum(i - 1, 0), ckv)),
                  pl.BlockSpec((SWA_BLOCK, kvw), lambda b, i: (b * nb + i, ckv)),
                  pl.BlockSpec((SWA_BLOCK, kvw), lambda b, i: (b * nb + jnp.minimum(i + 1, nb - 1), ckv)),
                  pl.BlockSpec(bias.shape, lambda b, i: (0, 0, 0))],
        out_specs=pl.BlockSpec((SWA_BLOCK, MIX), lambda b, i: (b * nb + i, 0)),
        out_shape=jax.ShapeDtypeStruct((T, MIX), BF),
        compiler_params=_params("parallel", "parallel"),
        name="swa_attention",
    )(sinks2, z, z, z, z, bias)


def _merge_kernel(oa_ref, ob_ref, oc_ref, od_ref, w_ref, g0_ref, g1_ref, g2_ref, g3_ref, out_ref):
    acc = None
    for b, (o_ref, g_ref) in enumerate(((oa_ref, g0_ref), (ob_ref, g1_ref), (oc_ref, g2_ref), (od_ref, g3_ref))):
        y = _sigmoid(g_ref[...].astype(F32)) * _dot(o_ref[...], w_ref[b])
        acc = y if acc is None else acc + y
    out_ref[...] = acc.astype(BF)


def _merge(outs, z, w_branch):
    T = z.shape[0]
    tm, tn = 512, 512
    o_spec = pl.BlockSpec((tm, MIX), lambda i, j: (i, 0))

    def gate(b):
        return pl.BlockSpec((tm, tn), lambda i, j: (i, (Z_GATE + b * D_MODEL) // tn + j))

    return pl.pallas_call(
        _merge_kernel,
        grid=(T // tm, D_MODEL // tn),
        in_specs=[o_spec, o_spec, o_spec, o_spec,
                  pl.BlockSpec((N_BRANCH, MIX, tn), lambda i, j: (0, 0, j)),
                  gate(0), gate(1), gate(2), gate(3)],
        out_specs=pl.BlockSpec((tm, tn), lambda i, j: (i, j)),
        out_shape=jax.ShapeDtypeStruct((T, D_MODEL), BF),
        compiler_params=_params("parallel", "parallel"),
        name="merge",
    )(*outs, w_branch, z, z, z, z)


PACK_ROWS = 8
PACK_LANES = 128
_HALF = PACK_ROWS * PACK_LANES


def _f32_bits(x):
    return lax.bitcast_convert_type(x, jnp.uint32)


def _pack_rows(ref, x, n):
    lo = _f32_bits(x[:, :_HALF].astype(BF).astype(F32)) >> 16
    hi = _f32_bits(x[:, _HALF:].astype(BF).astype(F32)) & jnp.uint32(0xFFFF0000)
    w = hi | lo
    for s in range(PACK_ROWS):
        ref[pl.ds(s, n, stride=PACK_ROWS), :] = w[:, s * PACK_LANES:(s + 1) * PACK_LANES]


def _unpack_rows(ref, n):
    lo, hi = [], []
    for s in range(PACK_ROWS):
        w = ref[pl.ds(s, n, stride=PACK_ROWS), :]
        lo.append(lax.bitcast_convert_type(w << 16, F32))
        hi.append(lax.bitcast_convert_type(w & jnp.uint32(0xFFFF0000), F32))
    return jnp.concatenate(lo + hi, axis=1)


def _out_proj_kernel(m_ref, w_ref, x_ref, g1_ref, n2_ref, sh_ref, sc_ref, wr_ref, x1_ref, h2_ref, hp_ref, lg_ref):
    x1 = x_ref[...] + g1_ref[0] * _dot(m_ref[...], w_ref[...])
    x1_ref[...] = x1
    h2 = _rms(x1, n2_ref[...], RMS_EPS) * (1.0 + sc_ref[0]) + sh_ref[0]
    h2_ref[...] = h2.astype(BF)
    _pack_rows(hp_ref, h2, h2.shape[0])
    lg_ref[...] = _dot_nt(wr_ref[...], h2, precision=lax.Precision.HIGHEST)


def _out_proj(merged, w_o, x, g1, n2, sh2, sc2, w_router_t, n_seq):
    T, D = x.shape
    tm = 256
    tpb = n_seq // tm
    row = lambda: pl.BlockSpec((tm, D), lambda i: (i, 0))
    per_batch = lambda: pl.BlockSpec((1, 1, D), lambda i: (i // tpb, 0, 0))
    return pl.pallas_call(
        _out_proj_kernel,
        grid=(T // tm,),
        in_specs=[row(), pl.BlockSpec((D, D), lambda i: (0, 0)), row(), per_batch(),
                  pl.BlockSpec((1, D), lambda i: (0, 0)), per_batch(), per_batch(),
                  pl.BlockSpec((N_EXPERTS, D), lambda i: (0, 0))],
        out_specs=[row(), row(), pl.BlockSpec((tm * PACK_ROWS, PACK_LANES), lambda i: (i, 0)),
                   pl.BlockSpec((N_EXPERTS, tm), lambda i: (0, i))],
        out_shape=[jax.ShapeDtypeStruct((T, D), F32), jax.ShapeDtypeStruct((T, D), BF),
                   jax.ShapeDtypeStruct((T * PACK_ROWS, PACK_LANES), jnp.uint32),
                   jax.ShapeDtypeStruct((N_EXPERTS, T), F32)],
        compiler_params=_params("parallel"),
        name="out_proj",
    )(merged, w_o, x, g1, n2, sh2, sc2, w_router_t)


def _route_kernel(lg_ref, b_ref, idx_ref, w_ref):
    s = _sigmoid(lg_ref[...])
    sel = s + b_ref[...]
    shape = sel.shape
    member = lax.broadcasted_iota(jnp.int32, shape, 1)
    eid = lax.broadcasted_iota(jnp.int32, shape, 0) * GROUP_SIZE + member

    def first_max(x, ids, axes, sentinel):
        m = x
        for a in axes:
            m = jnp.max(m, axis=a, keepdims=True)
        f = jnp.where(x == m, ids, sentinel)
        for a in axes:
            f = jnp.min(f, axis=a, keepdims=True)
        return f

    top1 = member == first_max(sel, member, (1,), GROUP_SIZE)
    m1 = jnp.max(sel, axis=1, keepdims=True)
    m2 = jnp.max(jnp.where(top1, NEG_INF, sel), axis=1, keepdims=True)
    score = m1 + m2
    gid = lax.broadcasted_iota(jnp.int32, score.shape, 0)
    chosen = jnp.zeros(score.shape, jnp.int32)
    for _ in range(TOPK_GROUPS):
        pick = gid == first_max(score, gid, (0,), N_GROUPS)
        chosen = jnp.where(pick, 1, chosen)
        score = jnp.where(pick, NEG_INF, score)
    cur = jnp.where(jnp.broadcast_to(chosen, shape) > 0, sel, NEG_INF)
    picked_w = []
    for k in range(TOP_K):
        f = first_max(cur, eid, (1, 0), N_EXPERTS)
        pick = eid == f
        wk = jnp.sum(jnp.sum(jnp.where(pick, s, 0.0), axis=1, keepdims=True), axis=0, keepdims=True)
        idx_ref[k:k + 1, :] = f[0]
        picked_w.append(wk[0])
        cur = jnp.where(pick, NEG_INF, cur)
    total = picked_w[0]
    for wk in picked_w[1:]:
        total = total + wk
    for k, wk in enumerate(picked_w):
        w_ref[k:k + 1, :] = wk / total * ROUTED_SCALE


def _route(logits_t, b_router):
    T = logits_t.shape[1]
    tt = 512
    lg = logits_t.reshape(N_GROUPS, GROUP_SIZE, T)
    return pl.pallas_call(
        _route_kernel,
        grid=(T // tt,),
        in_specs=[pl.BlockSpec((N_GROUPS, GROUP_SIZE, tt), lambda i: (0, 0, i)),
                  pl.BlockSpec((N_GROUPS, GROUP_SIZE, 1), lambda i: (0, 0, 0))],
        out_specs=[pl.BlockSpec((TOP_K, tt), lambda i: (0, i)), pl.BlockSpec((TOP_K, tt), lambda i: (0, i))],
        out_shape=[jax.ShapeDtypeStruct((TOP_K, T), jnp.int32), jax.ShapeDtypeStruct((TOP_K, T), F32)],
        compiler_params=_params("parallel"),
        name="route",
    )(lg, b_router.astype(F32).reshape(N_GROUPS, GROUP_SIZE, 1))


ROWS_PER_STEP = 512


def _row_copy(src_hbm, dst_hbm, s, d, sem):
    return pltpu.make_async_copy(src_hbm.at[pl.ds(pl.multiple_of(s * PACK_ROWS, PACK_ROWS), PACK_ROWS)],
                                 dst_hbm.at[pl.ds(pl.multiple_of(d * PACK_ROWS, PACK_ROWS), PACK_ROWS)], sem)


def _gather_kernel(idx_ref, src_hbm, dst_hbm, sem):
    base = pl.program_id(0) * ROWS_PER_STEP

    def start(r, c):
        _row_copy(src_hbm, dst_hbm, idx_ref[0, 0, r], base + r, sem).start()
        return c

    def wait(r, c):
        _row_copy(src_hbm, dst_hbm, 0, base + r, sem).wait()
        return c

    lax.fori_loop(0, ROWS_PER_STEP, start, 0)
    lax.fori_loop(0, ROWS_PER_STEP, wait, 0)


def _scatter_kernel(idx_ref, src_hbm, dst_hbm, sem):
    base = pl.program_id(0) * ROWS_PER_STEP

    def start(r, c):
        d = idx_ref[0, 0, r]

        @pl.when(d >= 0)
        def _():
            _row_copy(src_hbm, dst_hbm, base + r, d, sem).start()
        return c

    def wait(r, c):
        @pl.when(idx_ref[0, 0, r] >= 0)
        def _():
            _row_copy(src_hbm, dst_hbm, base + r, 0, sem).wait()
        return c

    lax.fori_loop(0, ROWS_PER_STEP, start, 0)
    lax.fori_loop(0, ROWS_PER_STEP, wait, 0)


def _move_rows(kernel, idx, src, n_out, name):
    steps = idx.shape[0] // ROWS_PER_STEP
    return pl.pallas_call(
        kernel,
        grid=(steps,),
        in_specs=[pl.BlockSpec((1, 1, ROWS_PER_STEP), lambda i: (i, 0, 0), memory_space=pltpu.SMEM),
                  pl.BlockSpec(memory_space=pl.ANY)],
        out_specs=pl.BlockSpec(memory_space=pl.ANY),
        out_shape=jax.ShapeDtypeStruct((n_out * PACK_ROWS, PACK_LANES), src.dtype),
        scratch_shapes=[pltpu.SemaphoreType.DMA(())],
        compiler_params=_params("arbitrary"),
        name=name,
    )(idx.reshape(steps, 1, ROWS_PER_STEP), src)


def _swiglu_tile(x, wg, wu, wd):
    a = _silu(_dot(x, wg)) * _dot(x, wu)
    return _dot(a.astype(BF), wd)


def _expert_kernel(blk_e_ref, x_ref, wg_ref, wu_ref, wd_ref, y_ref):
    x = _unpack_rows(x_ref, MOE_BLOCK).astype(BF)
    _pack_rows(y_ref, _swiglu_tile(x, wg_ref[0], wu_ref[0], wd_ref[0]), MOE_BLOCK)


def _experts(xs, blk_e, wg, wu, wd):
    n_blocks = blk_e.shape[0]
    D = wg.shape[1]
    rows = pl.BlockSpec((MOE_BLOCK * PACK_ROWS, PACK_LANES), lambda i, e: (i, 0))
    return pl.pallas_call(
        _expert_kernel,
        grid_spec=pltpu.PrefetchScalarGridSpec(
            num_scalar_prefetch=1,
            grid=(n_blocks,),
            in_specs=[rows,
                      pl.BlockSpec((1, D, EXPERT_DIM), lambda i, e: (e[i], 0, 0)),
                      pl.BlockSpec((1, D, EXPERT_DIM), lambda i, e: (e[i], 0, 0)),
                      pl.BlockSpec((1, EXPERT_DIM, D), lambda i, e: (e[i], 0, 0))],
            out_specs=rows),
        out_shape=jax.ShapeDtypeStruct(xs.shape, xs.dtype),
        compiler_params=_params("arbitrary"),
        name="experts",
    )(blk_e, xs, wg, wu, wd)


def _combine_kernel(final, x1_ref, h2_ref, w_ref, g2_ref, wg_ref, wu_ref, wd_ref, fg_ref, *rest):
    y_refs, out_ref = rest[:TOP_K], rest[TOP_K]
    w = w_ref[...]
    acc = _swiglu_tile(h2_ref[...], wg_ref[...], wu_ref[...], wd_ref[...])
    for k, y_ref in enumerate(y_refs):
        acc = acc + w[:, k:k + 1] * _unpack_rows(y_ref, acc.shape[0])
    x2 = x1_ref[...] + g2_ref[0] * acc
    out_ref[...] = _rms(x2, fg_ref[...], RMS_EPS) if final else x2


def _combine(x1, h2, w_tok, g2, wsg, wsu, wsd, final_g, ys, n_seq, final):
    T, D = x1.shape
    tm = 256
    tpb = n_seq // tm
    nt = T // tm
    row = lambda: pl.BlockSpec((tm, D), lambda i: (i, 0))
    full = lambda a: pl.BlockSpec(a.shape, lambda i: (0, 0))

    def slot(k):
        return pl.BlockSpec((tm * PACK_ROWS, PACK_LANES), lambda i: (k * nt + i, 0))

    return pl.pallas_call(
        functools.partial(_combine_kernel, final),
        grid=(nt,),
        in_specs=[row(), row(), pl.BlockSpec((tm, TOP_K), lambda i: (i, 0)),
                  pl.BlockSpec((1, 1, D), lambda i: (i // tpb, 0, 0)),
                  full(wsg), full(wsu), full(wsd), full(final_g)] + [slot(k) for k in range(TOP_K)],
        out_specs=row(),
        out_shape=jax.ShapeDtypeStruct((T, D), F32),
        compiler_params=_params("parallel"),
        name="combine",
    )(x1, h2, w_tok, g2, wsg, wsu, wsd, final_g, *([ys] * TOP_K))


def _dispatch_plan(idx_t):
    K, T = idx_t.shape
    TK = K * T
    n_blocks = TK // MOE_BLOCK + N_EXPERTS
    P = n_blocks * MOE_BLOCK
    e_flat = idx_t.reshape(TK)
    order = jnp.argsort(e_flat).astype(jnp.int32)
    e_sorted = e_flat[order]
    bounds = jnp.searchsorted(e_sorted, jnp.arange(N_EXPERTS + 1, dtype=jnp.int32), side="left").astype(jnp.int32)
    start, counts = bounds[:-1], bounds[1:] - bounds[:-1]
    padded = (counts + MOE_BLOCK - 1) // MOE_BLOCK * MOE_BLOCK
    p_end = jnp.cumsum(padded)
    p_start = p_end - padded
    blk_e = jnp.minimum(jnp.searchsorted(p_end, jnp.arange(n_blocks, dtype=jnp.int32) * MOE_BLOCK, side="right"),
                        N_EXPERTS - 1).astype(jnp.int32)
    row_e = jnp.repeat(blk_e, MOE_BLOCK)
    j = jnp.arange(P, dtype=jnp.int32) - p_start[row_e]
    valid = j < counts[row_e]
    flat = order[jnp.clip(start[row_e] + j, 0, TK - 1)]
    src_tok = jnp.where(valid, flat % T, 0).astype(jnp.int32)
    dst_slot = jnp.where(valid, flat, -1).astype(jnp.int32)
    return src_tok, dst_slot, blk_e


def _layer_weights(l, p):
    f = lambda a: a.astype(BF)
    w_in = f(jnp.take(p["w_in"][l], jnp.asarray(_IN_SRC), axis=1) * jnp.asarray(_IN_MUL)[None, :])
    wq, wk, wv = _mla_weights(p["mla_w_q_up"][l], p["mla_w_kv_up"][l])
    lam_init = 0.8 - 0.6 * math.exp(-0.3 * l)
    lam = (jnp.exp(jnp.sum(p["diff_lambda_q1"][l] * p["diff_lambda_k1"][l]))
           - jnp.exp(jnp.sum(p["diff_lambda_q2"][l] * p["diff_lambda_k2"][l])) + lam_init)
    diff_par = jnp.concatenate([jnp.asarray(_alibi_slopes(DIFF_HEADS) * LOG2E, F32),
                                jnp.stack([lam, jnp.asarray(1.0 - lam_init, F32)]).astype(F32),
                                jnp.zeros((2,), F32)])
    return dict(
        w_in=w_in, wq=wq, wk=wk, wv=wv,
        gq=p["mla_q_norm_g"][l][None, :], gkv=p["mla_kv_norm_g"][l][None, :],
        na_bias=_na_bias_table(p["na_rpb"][l]),
        diff_par=diff_par, subln_g=p["diff_subln_g"][l][None, :],
        sinks2=p["swa_sinks"][l].astype(F32) * LOG2E,
        w_branch=f(p["w_branch"][l]).reshape(N_BRANCH, MIX, D_MODEL), w_o=f(p["w_o"][l]),
        norm1_g=p["norm1_g"][l][None, :], norm2_g=p["norm2_g"][l][None, :],
        w_router_t=p["w_router"][l].T, b_router=p["b_router"][l],
        weg=f(p["w_exp_gate"][l]), weu=f(p["w_exp_up"][l]), wed=f(p["w_exp_down"][l]),
        wsg=f(p["w_sh_gate"][l]), wsu=f(p["w_sh_up"][l]), wsd=f(p["w_sh_down"][l]),
    )


def _trunk(x, mods, weights, final_g):
    B, N, D = x.shape
    T = B * N
    x = x.reshape(T, D)
    cs = _rope_table(N)
    n_layers = len(weights)
    for l, w in enumerate(weights):
        sh1, sc1, g1, sh2, sc2, g2 = [m[:, None, :] for m in jnp.split(mods[l], ADA_CHUNKS, axis=-1)]
        z = _in_proj(x, w["norm1_g"], sh1, sc1, w["w_in"], N)
        o_a = _na_attention(z, w["na_bias"], B, N)
        q, k, v = _mla_prep(z, cs, w["gq"], w["gkv"], w["wq"], w["wk"], w["wv"], N)
        o_b = _mla_flash(q, k, v, B, N)
        o_c = _diff_flash(z, w["diff_par"], w["subln_g"], B, N)
        o_d = _swa_attention(z, w["sinks2"], B, N)
        merged = _merge((o_a, o_b, o_c, o_d), z, w["w_branch"])
        x1, h2, h2_rows, logits_t = _out_proj(merged, w["w_o"], x, g1, w["norm2_g"], sh2, sc2, w["w_router_t"], N)
        idx_t, w_t = _route(logits_t, w["b_router"])
        src_tok, dst_slot, blk_e = _dispatch_plan(idx_t)
        xs = _move_rows(_gather_kernel, src_tok, h2_rows, src_tok.shape[0], "gather_rows")
        ys = _experts(xs, blk_e, w["weg"], w["weu"], w["wed"])
        ys_tok = _move_rows(_scatter_kernel, dst_slot, ys, TOP_K * T, "scatter_rows")
        x = _combine(x1, h2, w_t.T, g2, w["wsg"], w["wsu"], w["wsd"], final_g[None, :], ys_tok, N,
                     final=(l == n_layers - 1))
    return x.reshape(B, N, D)


def kernel(x_prompt, x_sample, c_prompt, c_sample, norm1_g, w_ada, b_ada, w_in, na_rpb, mla_q_norm_g, mla_w_q_up, mla_kv_norm_g, mla_w_kv_up, diff_lambda_q1, diff_lambda_k1, diff_lambda_q2, diff_lambda_k2, diff_subln_g, swa_sinks, w_branch, w_o, norm2_g, w_router, b_router, w_exp_gate, w_exp_up, w_exp_down, w_sh_gate, w_sh_up, w_sh_down, final_g):
    p = dict(norm1_g=norm1_g, w_in=w_in, na_rpb=na_rpb, mla_q_norm_g=mla_q_norm_g, mla_w_q_up=mla_w_q_up,
             mla_kv_norm_g=mla_kv_norm_g, mla_w_kv_up=mla_w_kv_up, diff_lambda_q1=diff_lambda_q1,
             diff_lambda_k1=diff_lambda_k1, diff_lambda_q2=diff_lambda_q2, diff_lambda_k2=diff_lambda_k2,
             diff_subln_g=diff_subln_g, swa_sinks=swa_sinks, w_branch=w_branch, w_o=w_o, norm2_g=norm2_g,
             w_router=w_router, b_router=b_router, w_exp_gate=w_exp_gate, w_exp_up=w_exp_up,
             w_exp_down=w_exp_down, w_sh_gate=w_sh_gate, w_sh_up=w_sh_up, w_sh_down=w_sh_down)
    n_layers = w_in.shape[0]
    bp, bs = c_prompt.shape[0], c_sample.shape[0]
    assert bp + bs <= 8
    c_all = jnp.concatenate([c_prompt, c_sample, jnp.zeros((8 - bp - bs, c_prompt.shape[1]), F32)], axis=0)
    mods = _ada(c_all, w_ada, b_ada)
    weights = [_layer_weights(l, p) for l in range(n_layers)]
    y_prompt = _trunk(x_prompt, mods[:, :bp], weights, final_g)
    y_sample = _trunk(x_sample, mods[:, bp:bp + bs], weights, final_g)
    return (y_prompt, y_sample)
```

```python
import functools
import math

import numpy as np
import jax
import jax.numpy as jnp
from jax import lax
from jax.experimental import pallas as pl
from jax.experimental.pallas import tpu as pltpu

BF = jnp.bfloat16
F32 = jnp.float32
LOG2E = 1.4426950408889634
NEG_INF = float("-inf")

VMEM_LIMIT_BYTES = 56 * 1024 * 1024

D_MODEL = 2048
GRID_W = 64
RMS_EPS = 1e-6
ADA_CHUNKS = 6
NA_HEADS, NA_DIM, NA_WIN_R, NA_WIN_C = 4, 128, 8, 16
MLA_HEADS, MLA_Q_RANK, MLA_KV_RANK, MLA_NOPE, MLA_ROPE, MLA_V = 4, 384, 256, 128, 64, 128
ROPE_BASE = 10000.0
DIFF_HEADS, DIFF_DIM, DIFF_EPS = 4, 64, 1e-5
SWA_HEADS, SWA_KV_HEADS, SWA_DIM, SWA_WINDOW = 8, 2, 64, 128
SWA_BLOCK = 128
MIX = 512
N_BRANCH = 4
N_EXPERTS, TOP_K, N_GROUPS, TOPK_GROUPS = 64, 8, 8, 4
GROUP_SIZE = N_EXPERTS // N_GROUPS
EXPERT_DIM, SHARED_DIM = 512, 512
ROUTED_SCALE = 2.5
MOE_BLOCK = 256

_O_QA, _O_KA, _O_VA = 0, 512, 1024
_O_QLAT, _O_KVLAT, _O_KROPE = 1536, 1920, 2176
_O_QC, _O_KC, _O_VC = 2240, 2752, 3264
_O_QD, _O_KD, _O_VD = 3776, 4288, 4416
_O_GATE = 4544
Z_QA, Z_KA, Z_VA = 0, 512, 1024
Z_QC, Z_KC, Z_VC = 1536, 2048, 2560
Z_QD = 3072
Z_KVD = 3584
Z_MLA = 3840
Z_MLA_W = 768
Z_GATE = 4608
Z_WIDTH = Z_GATE + N_BRANCH * D_MODEL


_ROPE_HALF = MLA_ROPE // 2
_IN_SEGMENTS = (
    (Z_QA, _O_QA, 512, NA_DIM ** -0.5 * LOG2E), (Z_KA, _O_KA, 512, 1.0), (Z_VA, _O_VA, 512, 1.0),
    (Z_QC, _O_QC, 512, DIFF_DIM ** -0.5 * LOG2E), (Z_KC, _O_KC, 512, 1.0), (Z_VC, _O_VC, 512, 1.0),
    (Z_QD, _O_QD, 512, SWA_DIM ** -0.5 * LOG2E),
    (Z_KVD, _O_KD, 128, 1.0), (Z_KVD + 128, _O_VD, 128, 1.0),
    (Z_MLA, _O_QLAT, MLA_Q_RANK, 1.0), (Z_MLA + 384, _O_KVLAT, MLA_KV_RANK, 1.0),
    (Z_MLA + 640, _O_KROPE, MLA_ROPE, 1.0),
    (Z_MLA + 704, _O_KROPE + _ROPE_HALF, _ROPE_HALF, -1.0), (Z_MLA + 704 + _ROPE_HALF, _O_KROPE, _ROPE_HALF, 1.0),
    (Z_GATE, _O_GATE, N_BRANCH * D_MODEL, 1.0),
)


def _in_proj_weight(w_in):
    parts, at = [], 0
    for dst, src, n, mul in _IN_SEGMENTS:
        assert dst == at
        seg = w_in[:, src:src + n]
        parts.append((seg if mul == 1.0 else seg * mul).astype(BF))
        at += n
    assert at == Z_WIDTH
    return jnp.concatenate(parts, axis=1)


def _params(*sem):
    return pltpu.CompilerParams(dimension_semantics=sem, vmem_limit_bytes=VMEM_LIMIT_BYTES)


def _dot(a, b):
    return jnp.dot(a, b, preferred_element_type=F32)


def _dot_nt(a, b, **kw):
    return lax.dot_general(a, b, (((1,), (1,)), ((), ())), preferred_element_type=F32, **kw)


def _sigmoid(x):
    return 1.0 / (1.0 + jnp.exp(-x))


def _silu(x):
    return x * _sigmoid(x)


def _rms(x, g, eps):
    return x * lax.rsqrt(jnp.mean(x * x, axis=-1, keepdims=True) + eps) * g


def _ada_kernel(c_ref, w_ref, b_ref, o_ref):
    a = _silu(c_ref[...]).astype(BF)
    o_ref[0] = _dot(a, w_ref[0].astype(BF)) + b_ref[0]


def _ada(c_all, w_ada, b_ada):
    L, D, W = w_ada.shape
    tn = 768
    return pl.pallas_call(
        _ada_kernel,
        grid=(L, W // tn),
        in_specs=[pl.BlockSpec((8, D), lambda l, j: (0, 0)),
                  pl.BlockSpec((1, D, tn), lambda l, j: (l, 0, j)),
                  pl.BlockSpec((1, 1, tn), lambda l, j: (l, 0, j))],
        out_specs=pl.BlockSpec((1, 8, tn), lambda l, j: (l, 0, j)),
        out_shape=jax.ShapeDtypeStruct((L, 8, W), F32),
        compiler_params=_params("parallel", "parallel"),
        name="ada",
    )(c_all, w_ada, b_ada.reshape(L, 1, W))


def _in_proj_kernel(x_ref, g_ref, sh_ref, sc_ref, w_ref, z_ref, h_ref):
    @pl.when(pl.program_id(1) == 0)
    def _():
        h = _rms(x_ref[...], g_ref[...], RMS_EPS) * (1.0 + sc_ref[0]) + sh_ref[0]
        h_ref[...] = h.astype(BF)

    z_ref[...] = _dot(h_ref[...], w_ref[...]).astype(BF)


def _in_proj(x, g, sh, sc, w, n_seq):
    T, D = x.shape
    tm, tn = 512, 1280
    tpb = n_seq // tm
    return pl.pallas_call(
        _in_proj_kernel,
        grid=(T // tm, Z_WIDTH // tn),
        in_specs=[pl.BlockSpec((tm, D), lambda i, j: (i, 0)),
                  pl.BlockSpec((1, D), lambda i, j: (0, 0)),
                  pl.BlockSpec((1, 1, D), lambda i, j: (i // tpb, 0, 0)),
                  pl.BlockSpec((1, 1, D), lambda i, j: (i // tpb, 0, 0)),
                  pl.BlockSpec((D, tn), lambda i, j: (0, j))],
        out_specs=pl.BlockSpec((tm, tn), lambda i, j: (i, j)),
        out_shape=jax.ShapeDtypeStruct((T, Z_WIDTH), BF),
        scratch_shapes=[pltpu.VMEM((tm, D), BF)],
        compiler_params=_params("parallel", "arbitrary"),
        name="in_proj",
    )(x, g, sh, sc, w)


NA_ROWS_PER_STEP = 8
NA_TOK = NA_ROWS_PER_STEP * GRID_W


def _na_bias_table(rpb):
    cols = np.arange(GRID_W)
    c0 = np.clip(cols - NA_WIN_C // 2, 0, GRID_W - NA_WIN_C)
    col_mask = (cols[None, :] >= c0[:, None]) & (cols[None, :] < c0[:, None] + NA_WIN_C)
    dc = np.clip(cols[None, :] - cols[:, None] + NA_WIN_C - 1, 0, 2 * NA_WIN_C - 2)
    dr = np.arange(NA_WIN_R)[:, None] + np.arange(NA_WIN_R)[None, :]
    tab = rpb.astype(F32)[:, dr[:, None, :, None], dc[None, :, None, :]]
    tab = jnp.where(col_mask[None, None, :, None, :], tab * LOG2E, NEG_INF)
    return jnp.transpose(tab, (1, 0, 2, 3, 4)).reshape(NA_WIN_R, NA_HEADS, GRID_W, NA_WIN_R * GRID_W)


def _na_kernel(rows, q_ref, kp_ref, kc_ref, kn_ref, vp_ref, vc_ref, vn_ref, bias_ref, o_ref, kbuf, vbuf):
    i = pl.program_id(1)
    kbuf[0:NA_TOK] = kp_ref[...]
    kbuf[NA_TOK:2 * NA_TOK] = kc_ref[...]
    kbuf[2 * NA_TOK:3 * NA_TOK] = kn_ref[...]
    vbuf[0:NA_TOK] = vp_ref[...]
    vbuf[NA_TOK:2 * NA_TOK] = vc_ref[...]
    vbuf[2 * NA_TOK:3 * NA_TOK] = vn_ref[...]
    win = NA_WIN_R * GRID_W
    for rl in range(NA_ROWS_PER_STEP):
        r = i * NA_ROWS_PER_STEP + rl
        r0 = jnp.clip(r - NA_WIN_R // 2, 0, rows - NA_WIN_R)
        variant = r0 - r + NA_WIN_R - 1
        off = pl.multiple_of((r0 - (i - 1) * NA_ROWS_PER_STEP) * GRID_W, GRID_W)
        kw = kbuf[pl.ds(off, win), :]
        vw = vbuf[pl.ds(off, win), :]
        q = q_ref[rl * GRID_W:(rl + 1) * GRID_W, :]
        for h in range(NA_HEADS):
            hs = slice(h * NA_DIM, (h + 1) * NA_DIM)
            s = _dot_nt(q[:, hs], kw[:, hs]) + bias_ref[variant, h]
            m = jnp.max(s, axis=1, keepdims=True)
            p = jnp.exp2(s - m)
            l = jnp.sum(p, axis=1, keepdims=True)
            o = _dot(p.astype(BF), vw[:, hs]) / l
            o_ref[rl * GRID_W:(rl + 1) * GRID_W, hs] = o.astype(BF)


def _na_attention(z, bias, batch, n_seq):
    T = z.shape[0]
    rows = n_seq // GRID_W
    nblk = rows // NA_ROWS_PER_STEP
    assert rows % NA_ROWS_PER_STEP == 0 and rows >= 2 * NA_ROWS_PER_STEP

    def cur(c):
        return pl.BlockSpec((NA_TOK, MIX), lambda b, i: (b * nblk + i, c))

    def prev(c):
        return pl.BlockSpec((NA_TOK, MIX), lambda b, i: (b * nblk + jnp.maximum(i - 1, 0), c))

    def nxt(c):
        return pl.BlockSpec((NA_TOK, MIX), lambda b, i: (b * nblk + jnp.minimum(i + 1, nblk - 1), c))

    ck, cv = Z_KA // MIX, Z_VA // MIX
    return pl.pallas_call(
        functools.partial(_na_kernel, rows),
        grid=(batch, nblk),
        in_specs=[cur(Z_QA // MIX), prev(ck), cur(ck), nxt(ck), prev(cv), cur(cv), nxt(cv),
                  pl.BlockSpec(bias.shape, lambda b, i: (0, 0, 0, 0))],
        out_specs=pl.BlockSpec((NA_TOK, MIX), lambda b, i: (b * nblk + i, 0)),
        out_shape=jax.ShapeDtypeStruct((T, MIX), BF),
        scratch_shapes=[pltpu.VMEM((3 * NA_TOK, MIX), BF), pltpu.VMEM((3 * NA_TOK, MIX), BF)],
        compiler_params=_params("parallel", "parallel"),
        name="na_attention",
    )(z, z, z, z, z, z, z, bias)


MLA_QK = 256
MLA_Q_SCALE = (MLA_NOPE + MLA_ROPE) ** -0.5 * LOG2E


def _mla_weights(w_q_up, w_kv_up):
    half = MLA_ROPE // 2
    wq = w_q_up.reshape(MLA_Q_RANK, MLA_HEADS, MLA_NOPE + MLA_ROPE)
    pe = wq[:, :, MLA_NOPE:]
    rot = jnp.concatenate([-pe[:, :, half:], pe[:, :, :half]], axis=-1)
    wq = jnp.concatenate([wq, rot], axis=-1).reshape(MLA_Q_RANK, MLA_HEADS * MLA_QK)
    wkv = w_kv_up.reshape(MLA_KV_RANK, MLA_HEADS, MLA_NOPE + MLA_V)
    wk = wkv[:, :, :MLA_NOPE].reshape(MLA_KV_RANK, MLA_HEADS * MLA_NOPE)
    wv = wkv[:, :, MLA_NOPE:].reshape(MLA_KV_RANK, MLA_HEADS * MLA_V)
    return wq.astype(BF), wk.astype(BF), wv.astype(BF)


def _rope_table(n):
    inv = 1.0 / (ROPE_BASE ** (jnp.arange(0, MLA_ROPE, 2, dtype=F32) / MLA_ROPE))
    ang = jnp.arange(n, dtype=F32)[:, None] * inv[None, :]
    c, s = jnp.cos(ang), jnp.sin(ang)
    return jnp.concatenate([c, c, s, s], axis=-1)


def _mla_prep_kernel(z_ref, cs_ref, gq_ref, gkv_ref, wq_ref, wk_ref, wv_ref, q_ref, k_ref, v_ref):
    z = z_ref[...].astype(F32)
    qn = _rms(z[:, :MLA_Q_RANK], gq_ref[...], RMS_EPS).astype(BF)
    kvn = _rms(z[:, MLA_Q_RANK:MLA_Q_RANK + MLA_KV_RANK], gkv_ref[...], RMS_EPS).astype(BF)
    q = _dot(qn, wq_ref[...])
    kn = _dot(kvn, wk_ref[...])
    v_ref[...] = _dot(kvn, wv_ref[...]).astype(BF)
    cs = cs_ref[...]
    keep = lax.broadcasted_iota(jnp.int32, cs.shape, 1) < MLA_ROPE

    def rope(g):
        t = g * cs
        return jnp.where(keep, t + pltpu.roll(t, MLA_ROPE, 1), 0.0)

    k_pe = rope(z[:, 640:768]).astype(BF)
    for h in range(MLA_HEADS):
        lo, mid, hi = h * MLA_QK, h * MLA_QK + MLA_NOPE, (h + 1) * MLA_QK
        q_ref[:, lo:mid] = (q[:, lo:mid] * MLA_Q_SCALE).astype(BF)
        q_ref[:, mid:hi] = (rope(q[:, mid:hi]) * MLA_Q_SCALE).astype(BF)
        k_ref[:, lo:mid] = kn[:, h * MLA_NOPE:(h + 1) * MLA_NOPE].astype(BF)
        k_ref[:, mid:hi] = k_pe


def _mla_prep(z, cs, gq, gkv, wq, wk, wv, n_seq):
    T = z.shape[0]
    tm = 512
    tpb = n_seq // tm

    def full(a):
        return pl.BlockSpec(a.shape, lambda i: (0, 0))

    return pl.pallas_call(
        _mla_prep_kernel,
        grid=(T // tm,),
        in_specs=[pl.BlockSpec((tm, Z_MLA_W), lambda i: (i, Z_MLA // Z_MLA_W)),
                  pl.BlockSpec((tm, 128), lambda i: (i % tpb, 0)),
                  full(gq), full(gkv), full(wq), full(wk), full(wv)],
        out_specs=[pl.BlockSpec((tm, MLA_HEADS * MLA_QK), lambda i: (i, 0)),
                   pl.BlockSpec((tm, MLA_HEADS * MLA_QK), lambda i: (i, 0)),
                   pl.BlockSpec((tm, MLA_HEADS * MLA_V), lambda i: (i, 0))],
        out_shape=[jax.ShapeDtypeStruct((T, MLA_HEADS * MLA_QK), BF),
                   jax.ShapeDtypeStruct((T, MLA_HEADS * MLA_QK), BF),
                   jax.ShapeDtypeStruct((T, MLA_HEADS * MLA_V), BF)],
        compiler_params=_params("parallel"),
        name="mla_prep",
    )(z, cs, gq, gkv, wq, wk, wv)


def _flash_init(m_ref, l_ref, acc_ref):
    m_ref[...] = jnp.full(m_ref.shape, NEG_INF, F32)
    l_ref[...] = jnp.zeros(l_ref.shape, F32)
    acc_ref[...] = jnp.zeros(acc_ref.shape, F32)


def _flash_step(s_t, v, m_ref, l_ref, acc_ref):
    m_prev = m_ref[...]
    m_new = jnp.maximum(m_prev, jnp.max(s_t, axis=0, keepdims=True))
    alpha = jnp.exp2(m_prev - m_new)
    p_t = jnp.exp2(s_t - m_new)
    l_ref[...] = alpha * l_ref[...] + jnp.sum(p_t, axis=0, keepdims=True)
    pv = lax.dot_general(v, p_t.astype(BF), (((0,), (0,)), ((), ())), preferred_element_type=F32)
    acc_ref[...] = alpha * acc_ref[...] + pv
    m_ref[...] = m_new


FLASH_TQ, FLASH_TK = 512, 1024


def _mla_flash_kernel(q_ref, k_ref, v_ref, o_ref, m_ref, l_ref, acc_ref):
    j = pl.program_id(2)

    @pl.when(j == 0)
    def _():
        _flash_init(m_ref, l_ref, acc_ref)

    for h in range(MLA_HEADS):
        qk = slice(h * MLA_QK, (h + 1) * MLA_QK)
        s_t = _dot_nt(k_ref[:, qk], q_ref[:, qk])
        _flash_step(s_t, v_ref[:, h * MLA_V:(h + 1) * MLA_V], m_ref.at[h], l_ref.at[h], acc_ref.at[h])

    @pl.when(j == pl.num_programs(2) - 1)
    def _():
        for h in range(MLA_HEADS):
            o_ref[:, h * MLA_V:(h + 1) * MLA_V] = (acc_ref[h] / l_ref[h]).T.astype(BF)


def _mla_flash(q, k, v, batch, n_seq):
    T = q.shape[0]
    tq, tk = FLASH_TQ, FLASH_TK
    nq, nk = n_seq // tq, n_seq // tk
    H = MLA_HEADS
    return pl.pallas_call(
        _mla_flash_kernel,
        grid=(batch, nq, nk),
        in_specs=[pl.BlockSpec((tq, H * MLA_QK), lambda b, i, j: (b * nq + i, 0)),
                  pl.BlockSpec((tk, H * MLA_QK), lambda b, i, j: (b * nk + j, 0)),
                  pl.BlockSpec((tk, H * MLA_V), lambda b, i, j: (b * nk + j, 0))],
        out_specs=pl.BlockSpec((tq, H * MLA_V), lambda b, i, j: (b * nq + i, 0)),
        out_shape=jax.ShapeDtypeStruct((T, H * MLA_V), BF),
        scratch_shapes=[pltpu.VMEM((H, 1, tq), F32), pltpu.VMEM((H, 1, tq), F32), pltpu.VMEM((H, MLA_V, tq), F32)],
        compiler_params=_params("parallel", "parallel", "arbitrary"),
        name="mla_flash",
    )(q, k, v)


DIFF_TQ, DIFF_TK = 512, 1024
DIFF_W = 2 * DIFF_DIM


def _diff_flash_kernel(par_ref, q_ref, k_ref, v_ref, d_ref, g_ref, o_ref, q1_ref, q2_ref, m_ref, l_ref, acc_ref):
    i, j = pl.program_id(1), pl.program_id(2)

    @pl.when(j == 0)
    def _():
        q = q_ref[...]
        first = lax.broadcasted_iota(jnp.int32, q.shape, 1) % DIFF_W < DIFF_DIM
        zero = jnp.zeros_like(q)
        q1_ref[...] = jnp.where(first, q, zero)
        q2_ref[...] = jnp.where(first, zero, q)
        _flash_init(m_ref, l_ref, acc_ref)

    dist = jnp.abs(d_ref[...] + (i * DIFF_TQ - j * DIFF_TK).astype(F32))
    for h in range(DIFF_HEADS):
        hs = slice(h * DIFF_W, (h + 1) * DIFF_W)
        k, v = k_ref[:, hs], v_ref[:, hs]
        bias = par_ref[h] * dist
        for c, qm_ref in enumerate((q1_ref, q2_ref)):
            n = 2 * h + c
            _flash_step(_dot_nt(k, qm_ref[:, hs]) - bias, v, m_ref.at[n], l_ref.at[n], acc_ref.at[n])

    @pl.when(j == pl.num_programs(2) - 1)
    def _():
        lam, out_scale = par_ref[DIFF_HEADS], par_ref[DIFF_HEADS + 1]
        for h in range(DIFF_HEADS):
            n = 2 * h
            o_t = acc_ref[n] / l_ref[n] - lam * (acc_ref[n + 1] / l_ref[n + 1])
            o_ref[:, h * DIFF_W:(h + 1) * DIFF_W] = (_rms(o_t.T, g_ref[...], DIFF_EPS) * out_scale).astype(BF)


def _diff_flash(z, par, subln_g, batch, n_seq):
    T = z.shape[0]
    tq, tk = DIFF_TQ, DIFF_TK
    nq, nk = n_seq // tq, n_seq // tk
    w = DIFF_HEADS * DIFF_W
    chains = 2 * DIFF_HEADS
    delta = (np.arange(tq)[None, :] - np.arange(tk)[:, None]).astype(np.float32)
    return pl.pallas_call(
        _diff_flash_kernel,
        grid=(batch, nq, nk),
        in_specs=[pl.BlockSpec(memory_space=pltpu.SMEM),
                  pl.BlockSpec((tq, w), lambda b, i, j: (b * nq + i, Z_QC // w)),
                  pl.BlockSpec((tk, w), lambda b, i, j: (b * nk + j, Z_KC // w)),
                  pl.BlockSpec((tk, w), lambda b, i, j: (b * nk + j, Z_VC // w)),
                  pl.BlockSpec((tk, tq), lambda b, i, j: (0, 0)),
                  pl.BlockSpec((1, DIFF_W), lambda b, i, j: (0, 0))],
        out_specs=pl.BlockSpec((tq, w), lambda b, i, j: (b * nq + i, 0)),
        out_shape=jax.ShapeDtypeStruct((T, w), BF),
        scratch_shapes=[pltpu.VMEM((tq, w), BF), pltpu.VMEM((tq, w), BF),
                        pltpu.VMEM((chains, 1, tq), F32), pltpu.VMEM((chains, 1, tq), F32),
                        pltpu.VMEM((chains, DIFF_W, tq), F32)],
        compiler_params=_params("parallel", "parallel", "arbitrary"),
        name="diff_flash",
    )(par, z, z, z, jnp.asarray(delta), subln_g)


def _alibi_slopes(n):
    return 2.0 ** (-8.0 * np.arange(1, n + 1) / n)


def _swa_bias_table():
    rel = np.arange(3 * SWA_BLOCK)[None, :] - SWA_BLOCK - np.arange(SWA_BLOCK)[:, None]
    dist = np.abs(rel).astype(np.float64)
    tab = -_alibi_slopes(SWA_HEADS)[:, None, None] * dist[None] * LOG2E
    tab = np.where((dist <= SWA_WINDOW)[None], tab, -np.inf)
    return tab.astype(np.float32)


def _swa_kernel(sink_ref, q_ref, kvp_ref, kvc_ref, kvn_ref, bias_ref, o_ref):
    i = pl.program_id(1)
    last = pl.num_programs(1) - 1
    kv = jnp.concatenate([kvp_ref[...], kvc_ref[...], kvn_ref[...]], axis=0)
    col = lax.broadcasted_iota(jnp.int32, (SWA_BLOCK, 3 * SWA_BLOCK), 1)
    outside = ((col < SWA_BLOCK) & (i == 0)) | ((col >= 2 * SWA_BLOCK) & (i == last))
    group = SWA_HEADS // SWA_KV_HEADS
    for kvh in range(SWA_KV_HEADS):
        k = kv[:, kvh * SWA_DIM:(kvh + 1) * SWA_DIM]
        v = kv[:, (SWA_KV_HEADS + kvh) * SWA_DIM:(SWA_KV_HEADS + kvh + 1) * SWA_DIM]
        for g in range(group):
            hh = kvh * group + g
            hs = slice(hh * SWA_DIM, (hh + 1) * SWA_DIM)
            s = _dot_nt(q_ref[:, hs], k) + bias_ref[hh]
            s = jnp.where(outside, NEG_INF, s)
            sink = sink_ref[hh]
            m = jnp.maximum(jnp.max(s, axis=1, keepdims=True), sink)
            e = jnp.exp2(s - m)
            denom = jnp.sum(e, axis=1, keepdims=True) + jnp.exp2(sink - m)
            o_ref[:, hs] = (_dot(e.astype(BF), v) / denom).astype(BF)


def _swa_attention(z, sinks2, batch, n_seq):
    T = z.shape[0]
    nb = n_seq // SWA_BLOCK
    kvw = 2 * SWA_KV_HEADS * SWA_DIM
    ckv = Z_KVD // kvw
    bias = jnp.asarray(_swa_bias_table())
    return pl.pallas_call(
        _swa_kernel,
        grid=(batch, nb),
        in_specs=[pl.BlockSpec(memory_space=pltpu.SMEM),
                  pl.BlockSpec((SWA_BLOCK, MIX), lambda b, i: (b * nb + i, Z_QD // MIX)),
                  pl.BlockSpec((SWA_BLOCK, kvw), lambda b, i: (b * nb + jnp.maximum(i - 1, 0), ckv)),
                  pl.BlockSpec((SWA_BLOCK, kvw), lambda b, i: (b * nb + i, ckv)),
                  pl.BlockSpec((SWA_BLOCK, kvw), lambda b, i: (b * nb + jnp.minimum(i + 1, nb - 1), ckv)),
                  pl.BlockSpec(bias.shape, lambda b, i: (0, 0, 0))],
        out_specs=pl.BlockSpec((SWA_BLOCK, MIX), lambda b, i: (b * nb + i, 0)),
        out_shape=jax.ShapeDtypeStruct((T, MIX), BF),
        compiler_params=_params("parallel", "parallel"),
        name="swa_attention",
    )(sinks2, z, z, z, z, bias)


def _merge_kernel(oa_ref, ob_ref, oc_ref, od_ref, w_ref, g0_ref, g1_ref, g2_ref, g3_ref, out_ref):
    acc = None
    for b, (o_ref, g_ref) in enumerate(((oa_ref, g0_ref), (ob_ref, g1_ref), (oc_ref, g2_ref), (od_ref, g3_ref))):
        y = _sigmoid(g_ref[...].astype(F32)) * _dot(o_ref[...], w_ref[b])
        acc = y if acc is None else acc + y
    out_ref[...] = acc.astype(BF)


def _merge(outs, z, w_branch):
    T = z.shape[0]
    tm, tn = 512, 512
    o_spec = pl.BlockSpec((tm, MIX), lambda i, j: (i, 0))

    def gate(b):
        return pl.BlockSpec((tm, tn), lambda i, j: (i, (Z_GATE + b * D_MODEL) // tn + j))

    return pl.pallas_call(
        _merge_kernel,
        grid=(T // tm, D_MODEL // tn),
        in_specs=[o_spec, o_spec, o_spec, o_spec,
                  pl.BlockSpec((N_BRANCH, MIX, tn), lambda i, j: (0, 0, j)),
                  gate(0), gate(1), gate(2), gate(3)],
        out_specs=pl.BlockSpec((tm, tn), lambda i, j: (i, j)),
        out_shape=jax.ShapeDtypeStruct((T, D_MODEL), BF),
        compiler_params=_params("parallel", "parallel"),
        name="merge",
    )(*outs, w_branch, z, z, z, z)


PACK_ROWS = 8
PACK_LANES = 128
_HALF = PACK_ROWS * PACK_LANES


def _f32_bits(x):
    return lax.bitcast_convert_type(x, jnp.uint32)


def _pack_rows(ref, x, n):
    lo = _f32_bits(x[:, :_HALF].astype(BF).astype(F32)) >> 16
    hi = _f32_bits(x[:, _HALF:].astype(BF).astype(F32)) & jnp.uint32(0xFFFF0000)
    w = hi | lo
    for s in range(PACK_ROWS):
        ref[pl.ds(s, n, stride=PACK_ROWS), :] = w[:, s * PACK_LANES:(s + 1) * PACK_LANES]


def _unpack_rows(ref, n):
    lo, hi = [], []
    for s in range(PACK_ROWS):
        w = ref[pl.ds(s, n, stride=PACK_ROWS), :]
        lo.append(lax.bitcast_convert_type(w << 16, F32))
        hi.append(lax.bitcast_convert_type(w & jnp.uint32(0xFFFF0000), F32))
    return jnp.concatenate(lo + hi, axis=1)


def _out_proj_kernel(m_ref, w_ref, x_ref, g1_ref, n2_ref, sh_ref, sc_ref, wr_ref, x1_ref, h2_ref, hp_ref, lg_ref):
    x1 = x_ref[...] + g1_ref[0] * _dot(m_ref[...], w_ref[...])
    x1_ref[...] = x1
    h2 = _rms(x1, n2_ref[...], RMS_EPS) * (1.0 + sc_ref[0]) + sh_ref[0]
    h2_ref[...] = h2.astype(BF)
    _pack_rows(hp_ref, h2, h2.shape[0])
    lg_ref[...] = _dot_nt(wr_ref[...], h2, precision=lax.Precision.HIGHEST)


def _out_proj(merged, w_o, x, g1, n2, sh2, sc2, w_router_t, n_seq):
    T, D = x.shape
    tm = 256
    tpb = n_seq // tm
    row = lambda: pl.BlockSpec((tm, D), lambda i: (i, 0))
    per_batch = lambda: pl.BlockSpec((1, 1, D), lambda i: (i // tpb, 0, 0))
    return pl.pallas_call(
        _out_proj_kernel,
        grid=(T // tm,),
        in_specs=[row(), pl.BlockSpec((D, D), lambda i: (0, 0)), row(), per_batch(),
                  pl.BlockSpec((1, D), lambda i: (0, 0)), per_batch(), per_batch(),
                  pl.BlockSpec((N_EXPERTS, D), lambda i: (0, 0))],
        out_specs=[row(), row(), pl.BlockSpec((tm * PACK_ROWS, PACK_LANES), lambda i: (i, 0)),
                   pl.BlockSpec((N_EXPERTS, tm), lambda i: (0, i))],
        out_shape=[jax.ShapeDtypeStruct((T, D), F32), jax.ShapeDtypeStruct((T, D), BF),
                   jax.ShapeDtypeStruct((T * PACK_ROWS, PACK_LANES), jnp.uint32),
                   jax.ShapeDtypeStruct((N_EXPERTS, T), F32)],
        compiler_params=_params("parallel"),
        name="out_proj",
    )(merged, w_o, x, g1, n2, sh2, sc2, w_router_t)


def _route_kernel(lg_ref, b_ref, idx_ref, w_ref):
    s = _sigmoid(lg_ref[...])
    sel = s + b_ref[...]
    shape = sel.shape
    member = lax.broadcasted_iota(jnp.int32, shape, 1)
    eid = lax.broadcasted_iota(jnp.int32, shape, 0) * GROUP_SIZE + member

    def first_max(x, ids, axes, sentinel):
        m = x
        for a in axes:
            m = jnp.max(m, axis=a, keepdims=True)
        f = jnp.where(x == m, ids, sentinel)
        for a in axes:
            f = jnp.min(f, axis=a, keepdims=True)
        return f

    top1 = member == first_max(sel, member, (1,), GROUP_SIZE)
    m1 = jnp.max(sel, axis=1, keepdims=True)
    m2 = jnp.max(jnp.where(top1, NEG_INF, sel), axis=1, keepdims=True)
    score = m1 + m2
    gid = lax.broadcasted_iota(jnp.int32, score.shape, 0)
    chosen = jnp.zeros(score.shape, jnp.int32)
    for _ in range(TOPK_GROUPS):
        pick = gid == first_max(score, gid, (0,), N_GROUPS)
        chosen = jnp.where(pick, 1, chosen)
        score = jnp.where(pick, NEG_INF, score)
    cur = jnp.where(jnp.broadcast_to(chosen, shape) > 0, sel, NEG_INF)
    picked_w = []
    for k in range(TOP_K):
        f = first_max(cur, eid, (1, 0), N_EXPERTS)
        pick = eid == f
        wk = jnp.sum(jnp.sum(jnp.where(pick, s, 0.0), axis=1, keepdims=True), axis=0, keepdims=True)
        idx_ref[k:k + 1, :] = f[0]
        picked_w.append(wk[0])
        cur = jnp.where(pick, NEG_INF, cur)
    total = picked_w[0]
    for wk in picked_w[1:]:
        total = total + wk
    for k, wk in enumerate(picked_w):
        w_ref[k:k + 1, :] = wk / total * ROUTED_SCALE


def _route(logits_t, b_router):
    T = logits_t.shape[1]
    tt = 512
    lg = logits_t.reshape(N_GROUPS, GROUP_SIZE, T)
    return pl.pallas_call(
        _route_kernel,
        grid=(T // tt,),
        in_specs=[pl.BlockSpec((N_GROUPS, GROUP_SIZE, tt), lambda i: (0, 0, i)),
                  pl.BlockSpec((N_GROUPS, GROUP_SIZE, 1), lambda i: (0, 0, 0))],
        out_specs=[pl.BlockSpec((TOP_K, tt), lambda i: (0, i)), pl.BlockSpec((TOP_K, tt), lambda i: (0, i))],
        out_shape=[jax.ShapeDtypeStruct((TOP_K, T), jnp.int32), jax.ShapeDtypeStruct((TOP_K, T), F32)],
        compiler_params=_params("parallel"),
        name="route",
    )(lg, b_router.astype(F32).reshape(N_GROUPS, GROUP_SIZE, 1))


def _swiglu_tile(x, wg, wu, wd):
    a = _silu(_dot(x, wg)) * _dot(x, wu)
    return _dot(a.astype(BF), wd)


BLOCK_PACKED_ROWS = MOE_BLOCK * PACK_ROWS
DMA_ISSUE_UNROLL = 8


def _token_rows(ref, t):
    return ref.at[pl.ds(pl.multiple_of(t * PACK_ROWS, PACK_ROWS), PACK_ROWS)]


def _experts_kernel(n_blocks, blk_e_ref, src0_ref, src_next_ref, dst_ref, h_hbm, wg_ref, wu_ref, wd_ref, y_hbm,
                    xbuf, ybuf, gsem, ssem):
    i = pl.program_id(0)
    slot = lax.rem(i, 2)
    other = 1 - slot

    def start_gather(idx_ref, s):
        def body(r, c):
            pltpu.make_async_copy(_token_rows(h_hbm, idx_ref[0, 0, r]), _token_rows(xbuf.at[s], r), gsem.at[s]).start()
            return c
        lax.fori_loop(0, MOE_BLOCK, body, 0, unroll=DMA_ISSUE_UNROLL)

    def wait_gather(s):
        pltpu.make_async_copy(h_hbm.at[pl.ds(0, BLOCK_PACKED_ROWS)], xbuf.at[s], gsem.at[s]).wait()

    def wait_scatter(s):
        pltpu.make_async_copy(ybuf.at[s], y_hbm.at[pl.ds(0, BLOCK_PACKED_ROWS)], ssem.at[s]).wait()

    @pl.when(i == 0)
    def _():
        start_gather(src0_ref, 0)

    @pl.when(i + 1 < n_blocks)
    def _():
        start_gather(src_next_ref, other)

    wait_gather(slot)
    x = _unpack_rows(xbuf.at[slot], MOE_BLOCK).astype(BF)
    y = _swiglu_tile(x, wg_ref[0], wu_ref[0], wd_ref[0])

    @pl.when(i >= 2)
    def _():
        wait_scatter(slot)

    _pack_rows(ybuf.at[slot], y, MOE_BLOCK)

    def scatter(r, c):
        pltpu.make_async_copy(_token_rows(ybuf.at[slot], r), _token_rows(y_hbm, dst_ref[0, 0, r]), ssem.at[slot]).start()
        return c
    lax.fori_loop(0, MOE_BLOCK, scatter, 0, unroll=DMA_ISSUE_UNROLL)

    @pl.when(i == n_blocks - 1)
    def _():
        wait_scatter(slot)
        if n_blocks > 1:
            wait_scatter(other)


def _experts_fused(h_rows, src_tok, dst_slot, blk_e, wg, wu, wd):
    n_blocks = blk_e.shape[0]
    P = n_blocks * MOE_BLOCK
    D = wg.shape[1]
    idx_block = lambda f: pl.BlockSpec((1, 1, MOE_BLOCK), f, memory_space=pltpu.SMEM)
    buf = lambda: pltpu.VMEM((2, BLOCK_PACKED_ROWS, PACK_LANES), jnp.uint32)
    return pl.pallas_call(
        functools.partial(_experts_kernel, n_blocks),
        grid_spec=pltpu.PrefetchScalarGridSpec(
            num_scalar_prefetch=1,
            grid=(n_blocks,),
            in_specs=[idx_block(lambda i, e: (0, 0, 0)),
                      idx_block(lambda i, e: (jnp.minimum(i + 1, n_blocks - 1), 0, 0)),
                      idx_block(lambda i, e: (i, 0, 0)),
                      pl.BlockSpec(memory_space=pl.ANY),
                      pl.BlockSpec((1, D, EXPERT_DIM), lambda i, e: (e[i], 0, 0)),
                      pl.BlockSpec((1, D, EXPERT_DIM), lambda i, e: (e[i], 0, 0)),
                      pl.BlockSpec((1, EXPERT_DIM, D), lambda i, e: (e[i], 0, 0))],
            out_specs=pl.BlockSpec(memory_space=pl.ANY),
            scratch_shapes=[buf(), buf(), pltpu.SemaphoreType.DMA((2,)), pltpu.SemaphoreType.DMA((2,))]),
        out_shape=jax.ShapeDtypeStruct((P * PACK_ROWS, PACK_LANES), jnp.uint32),
        compiler_params=_params("arbitrary"),
        name="experts",
    )(blk_e, *([src_tok.reshape(n_blocks, 1, MOE_BLOCK)] * 2), dst_slot.reshape(n_blocks, 1, MOE_BLOCK),
      h_rows, wg, wu, wd)


def _combine_kernel(final, x1_ref, h2_ref, w_ref, g2_ref, wg_ref, wu_ref, wd_ref, fg_ref, *rest):
    y_refs, out_ref = rest[:TOP_K], rest[TOP_K]
    w = w_ref[...]
    acc = _swiglu_tile(h2_ref[...], wg_ref[...], wu_ref[...], wd_ref[...])
    for k, y_ref in enumerate(y_refs):
        acc = acc + w[:, k:k + 1] * _unpack_rows(y_ref, acc.shape[0])
    x2 = x1_ref[...] + g2_ref[0] * acc
    out_ref[...] = _rms(x2, fg_ref[...], RMS_EPS) if final else x2


def _combine(x1, h2, w_tok, g2, wsg, wsu, wsd, final_g, ys, n_seq, final):
    T, D = x1.shape
    tm = 256
    tpb = n_seq // tm
    nt = T // tm
    row = lambda: pl.BlockSpec((tm, D), lambda i: (i, 0))
    full = lambda a: pl.BlockSpec(a.shape, lambda i: (0, 0))

    def slot(k):
        return pl.BlockSpec((tm * PACK_ROWS, PACK_LANES), lambda i: (k * nt + i, 0))

    return pl.pallas_call(
        functools.partial(_combine_kernel, final),
        grid=(nt,),
        in_specs=[row(), row(), pl.BlockSpec((tm, TOP_K), lambda i: (i, 0)),
                  pl.BlockSpec((1, 1, D), lambda i: (i // tpb, 0, 0)),
                  full(wsg), full(wsu), full(wsd), full(final_g)] + [slot(k) for k in range(TOP_K)],
        out_specs=row(),
        out_shape=jax.ShapeDtypeStruct((T, D), F32),
        compiler_params=_params("parallel"),
        name="combine",
    )(x1, h2, w_tok, g2, wsg, wsu, wsd, final_g, *([ys] * TOP_K))


def _dispatch_plan(idx_t):
    K, T = idx_t.shape
    TK = K * T
    n_blocks = TK // MOE_BLOCK + N_EXPERTS
    i32 = jnp.int32
    e_flat = idx_t.reshape(TK)
    order = jnp.argsort(e_flat).astype(i32)
    experts = jnp.arange(N_EXPERTS, dtype=i32)
    counts = jnp.sum((e_flat[None, :] == experts[:, None]).astype(i32), axis=1)
    start = jnp.cumsum(counts) - counts
    padded = (counts + MOE_BLOCK - 1) // MOE_BLOCK * MOE_BLOCK
    p_end = jnp.cumsum(padded)
    p_start = p_end - padded
    pads = padded - counts
    pad_before = jnp.cumsum(pads) - pads
    blk_first = jnp.arange(n_blocks, dtype=i32) * MOE_BLOCK
    blk_e = jnp.minimum(jnp.sum((p_end[None, :] <= blk_first[:, None]).astype(i32), axis=1), N_EXPERTS - 1)
    j = (blk_first - p_start[blk_e])[:, None] + jnp.arange(MOE_BLOCK, dtype=i32)[None, :]
    cnt = counts[blk_e][:, None]
    valid = j < cnt
    flat = order[jnp.clip(start[blk_e][:, None] + j, 0, TK - 1)]
    src_tok = jnp.where(valid, flat % T, 0).astype(i32)
    dst_slot = jnp.where(valid, flat, TK + pad_before[blk_e][:, None] + j - cnt).astype(i32)
    return src_tok, dst_slot, blk_e.astype(i32)


def _layer_weights(l, p):
    f = lambda a: a.astype(BF)
    w_in = _in_proj_weight(p["w_in"][l])
    wq, wk, wv = _mla_weights(p["mla_w_q_up"][l], p["mla_w_kv_up"][l])
    lam_init = 0.8 - 0.6 * math.exp(-0.3 * l)
    lam = (jnp.exp(jnp.sum(p["diff_lambda_q1"][l] * p["diff_lambda_k1"][l]))
           - jnp.exp(jnp.sum(p["diff_lambda_q2"][l] * p["diff_lambda_k2"][l])) + lam_init)
    diff_par = jnp.concatenate([jnp.asarray(_alibi_slopes(DIFF_HEADS) * LOG2E, F32),
                                jnp.stack([lam, jnp.asarray(1.0 - lam_init, F32)]).astype(F32),
                                jnp.zeros((2,), F32)])
    return dict(
        w_in=w_in, wq=wq, wk=wk, wv=wv,
        gq=p["mla_q_norm_g"][l][None, :], gkv=p["mla_kv_norm_g"][l][None, :],
        na_bias=_na_bias_table(p["na_rpb"][l]),
        diff_par=diff_par, subln_g=p["diff_subln_g"][l][None, :],
        sinks2=p["swa_sinks"][l].astype(F32) * LOG2E,
        w_branch=f(p["w_branch"][l]).reshape(N_BRANCH, MIX, D_MODEL), w_o=f(p["w_o"][l]),
        norm1_g=p["norm1_g"][l][None, :], norm2_g=p["norm2_g"][l][None, :],
        w_router_t=p["w_router"][l].T, b_router=p["b_router"][l],
        weg=f(p["w_exp_gate"][l]), weu=f(p["w_exp_up"][l]), wed=f(p["w_exp_down"][l]),
        wsg=f(p["w_sh_gate"][l]), wsu=f(p["w_sh_up"][l]), wsd=f(p["w_sh_down"][l]),
    )


def _trunk(x, mods, weights, final_g):
    B, N, D = x.shape
    T = B * N
    x = x.reshape(T, D)
    cs = _rope_table(N)
    n_layers = len(weights)
    for l, w in enumerate(weights):
        sh1, sc1, g1, sh2, sc2, g2 = [m[:, None, :] for m in jnp.split(mods[l], ADA_CHUNKS, axis=-1)]
        z = _in_proj(x, w["norm1_g"], sh1, sc1, w["w_in"], N)
        o_a = _na_attention(z, w["na_bias"], B, N)
        q, k, v = _mla_prep(z, cs, w["gq"], w["gkv"], w["wq"], w["wk"], w["wv"], N)
        o_b = _mla_flash(q, k, v, B, N)
        o_c = _diff_flash(z, w["diff_par"], w["subln_g"], B, N)
        o_d = _swa_attention(z, w["sinks2"], B, N)
        merged = _merge((o_a, o_b, o_c, o_d), z, w["w_branch"])
        x1, h2, h2_rows, logits_t = _out_proj(merged, w["w_o"], x, g1, w["norm2_g"], sh2, sc2, w["w_router_t"], N)
        idx_t, w_t = _route(logits_t, w["b_router"])
        src_tok, dst_slot, blk_e = _dispatch_plan(idx_t)
        ys_tok = _experts_fused(h2_rows, src_tok, dst_slot, blk_e, w["weg"], w["weu"], w["wed"])
        x = _combine(x1, h2, w_t.T, g2, w["wsg"], w["wsu"], w["wsd"], final_g[None, :], ys_tok, N,
                     final=(l == n_layers - 1))
    return x.reshape(B, N, D)


def kernel(x_prompt, x_sample, c_prompt, c_sample, norm1_g, w_ada, b_ada, w_in, na_rpb, mla_q_norm_g, mla_w_q_up, mla_kv_norm_g, mla_w_kv_up, diff_lambda_q1, diff_lambda_k1, diff_lambda_q2, diff_lambda_k2, diff_subln_g, swa_sinks, w_branch, w_o, norm2_g, w_router, b_router, w_exp_gate, w_exp_up, w_exp_down, w_sh_gate, w_sh_up, w_sh_down, final_g):
    p = dict(norm1_g=norm1_g, w_in=w_in, na_rpb=na_rpb, mla_q_norm_g=mla_q_norm_g, mla_w_q_up=mla_w_q_up,
             mla_kv_norm_g=mla_kv_norm_g, mla_w_kv_up=mla_w_kv_up, diff_lambda_q1=diff_lambda_q1,
             diff_lambda_k1=diff_lambda_k1, diff_lambda_q2=diff_lambda_q2, diff_lambda_k2=diff_lambda_k2,
             diff_subln_g=diff_subln_g, swa_sinks=swa_sinks, w_branch=w_branch, w_o=w_o, norm2_g=norm2_g,
             w_router=w_router, b_router=b_router, w_exp_gate=w_exp_gate, w_exp_up=w_exp_up,
             w_exp_down=w_exp_down, w_sh_gate=w_sh_gate, w_sh_up=w_sh_up, w_sh_down=w_sh_down)
    n_layers = w_in.shape[0]
    bp, bs = c_prompt.shape[0], c_sample.shape[0]
    assert bp + bs <= 8
    c_all = jnp.concatenate([c_prompt, c_sample, jnp.zeros((8 - bp - bs, c_prompt.shape[1]), F32)], axis=0)
    mods = _ada(c_all, w_ada, b_ada)
    weights = [_layer_weights(l, p) for l in range(n_layers)]
    y_prompt = _trunk(x_prompt, mods[:, :bp], weights, final_g)
    y_sample = _trunk(x_sample, mods[:, bp:bp + bs], weights, final_g)
    return (y_prompt, y_sample)
```

```python
import functools
import math

import numpy as np
import jax
import jax.numpy as jnp
from jax import lax
from jax.experimental import pallas as pl
from jax.experimental.pallas import tpu as pltpu

BF = jnp.bfloat16
F32 = jnp.float32
LOG2E = 1.4426950408889634
NEG_INF = float("-inf")

VMEM_LIMIT_BYTES = 56 * 1024 * 1024

D_MODEL = 2048
GRID_W = 64
RMS_EPS = 1e-6
ADA_CHUNKS = 6
NA_HEADS, NA_DIM, NA_WIN_R, NA_WIN_C = 4, 128, 8, 16
MLA_HEADS, MLA_Q_RANK, MLA_KV_RANK, MLA_NOPE, MLA_ROPE, MLA_V = 4, 384, 256, 128, 64, 128
ROPE_BASE = 10000.0
DIFF_HEADS, DIFF_DIM, DIFF_EPS = 4, 64, 1e-5
SWA_HEADS, SWA_KV_HEADS, SWA_DIM, SWA_WINDOW = 8, 2, 64, 128
SWA_BLOCK = 128
MIX = 512
N_BRANCH = 4
N_EXPERTS, TOP_K, N_GROUPS, TOPK_GROUPS = 64, 8, 8, 4
GROUP_SIZE = N_EXPERTS // N_GROUPS
EXPERT_DIM, SHARED_DIM = 512, 512
ROUTED_SCALE = 2.5
MOE_BLOCK = 256

_O_QA, _O_KA, _O_VA = 0, 512, 1024
_O_QLAT, _O_KVLAT, _O_KROPE = 1536, 1920, 2176
_O_QC, _O_KC, _O_VC = 2240, 2752, 3264
_O_QD, _O_KD, _O_VD = 3776, 4288, 4416
_O_GATE = 4544
Z_QA, Z_KA, Z_VA = 0, 512, 1024
Z_QC, Z_KC, Z_VC = 1536, 2048, 2560
Z_QD = 3072
Z_KVD = 3584
Z_MLA = 3840
Z_MLA_W = 768
Z_GATE = 4608
Z_WIDTH = Z_GATE + N_BRANCH * D_MODEL


_ROPE_HALF = MLA_ROPE // 2
_IN_SEGMENTS = (
    (Z_QA, _O_QA, 512, NA_DIM ** -0.5 * LOG2E), (Z_KA, _O_KA, 512, 1.0), (Z_VA, _O_VA, 512, 1.0),
    (Z_QC, _O_QC, 512, DIFF_DIM ** -0.5 * LOG2E), (Z_KC, _O_KC, 512, 1.0), (Z_VC, _O_VC, 512, 1.0),
    (Z_QD, _O_QD, 512, SWA_DIM ** -0.5 * LOG2E),
    (Z_KVD, _O_KD, 128, 1.0), (Z_KVD + 128, _O_VD, 128, 1.0),
    (Z_MLA, _O_QLAT, MLA_Q_RANK, 1.0), (Z_MLA + 384, _O_KVLAT, MLA_KV_RANK, 1.0),
    (Z_MLA + 640, _O_KROPE, MLA_ROPE, 1.0),
    (Z_MLA + 704, _O_KROPE + _ROPE_HALF, _ROPE_HALF, -1.0), (Z_MLA + 704 + _ROPE_HALF, _O_KROPE, _ROPE_HALF, 1.0),
    (Z_GATE, _O_GATE, N_BRANCH * D_MODEL, 1.0),
)


def _in_proj_weight(w_in):
    parts, at = [], 0
    for dst, src, n, mul in _IN_SEGMENTS:
        assert dst == at
        seg = w_in[:, src:src + n]
        parts.append((seg if mul == 1.0 else seg * mul).astype(BF))
        at += n
    assert at == Z_WIDTH
    return jnp.concatenate(parts, axis=1)


def _params(*sem):
    return pltpu.CompilerParams(dimension_semantics=sem, vmem_limit_bytes=VMEM_LIMIT_BYTES)


def _dot(a, b):
    return jnp.dot(a, b, preferred_element_type=F32)


def _dot_nt(a, b, **kw):
    return lax.dot_general(a, b, (((1,), (1,)), ((), ())), preferred_element_type=F32, **kw)


def _sigmoid(x):
    return 1.0 / (1.0 + jnp.exp(-x))


def _silu(x):
    return x * _sigmoid(x)


def _rms(x, g, eps):
    return x * lax.rsqrt(jnp.mean(x * x, axis=-1, keepdims=True) + eps) * g


def _ada_kernel(c_ref, w_ref, b_ref, o_ref):
    a = _silu(c_ref[...]).astype(BF)
    o_ref[0] = _dot(a, w_ref[0].astype(BF)) + b_ref[0]


def _ada(c_all, w_ada, b_ada):
    L, D, W = w_ada.shape
    tn = 768
    return pl.pallas_call(
        _ada_kernel,
        grid=(L, W // tn),
        in_specs=[pl.BlockSpec((8, D), lambda l, j: (0, 0)),
                  pl.BlockSpec((1, D, tn), lambda l, j: (l, 0, j)),
                  pl.BlockSpec((1, 1, tn), lambda l, j: (l, 0, j))],
        out_specs=pl.BlockSpec((1, 8, tn), lambda l, j: (l, 0, j)),
        out_shape=jax.ShapeDtypeStruct((L, 8, W), F32),
        compiler_params=_params("parallel", "parallel"),
        name="ada",
    )(c_all, w_ada, b_ada.reshape(L, 1, W))


def _in_proj_kernel(x_ref, g_ref, sh_ref, sc_ref, w_ref, z_ref, h_ref):
    @pl.when(pl.program_id(1) == 0)
    def _():
        h = _rms(x_ref[...], g_ref[...], RMS_EPS) * (1.0 + sc_ref[0]) + sh_ref[0]
        h_ref[...] = h.astype(BF)

    z_ref[...] = _dot(h_ref[...], w_ref[...]).astype(BF)


def _in_proj(x, g, sh, sc, w, n_seq):
    T, D = x.shape
    tm, tn = 512, 1280
    tpb = n_seq // tm
    return pl.pallas_call(
        _in_proj_kernel,
        grid=(T // tm, Z_WIDTH // tn),
        in_specs=[pl.BlockSpec((tm, D), lambda i, j: (i, 0)),
                  pl.BlockSpec((1, D), lambda i, j: (0, 0)),
                  pl.BlockSpec((1, 1, D), lambda i, j: (i // tpb, 0, 0)),
                  pl.BlockSpec((1, 1, D), lambda i, j: (i // tpb, 0, 0)),
                  pl.BlockSpec((D, tn), lambda i, j: (0, j))],
        out_specs=pl.BlockSpec((tm, tn), lambda i, j: (i, j)),
        out_shape=jax.ShapeDtypeStruct((T, Z_WIDTH), BF),
        scratch_shapes=[pltpu.VMEM((tm, D), BF)],
        compiler_params=_params("parallel", "arbitrary"),
        name="in_proj",
    )(x, g, sh, sc, w)


NA_ROWS_PER_STEP = 8
NA_TOK = NA_ROWS_PER_STEP * GRID_W


def _na_bias_table(rpb):
    cols = np.arange(GRID_W)
    c0 = np.clip(cols - NA_WIN_C // 2, 0, GRID_W - NA_WIN_C)
    col_mask = (cols[None, :] >= c0[:, None]) & (cols[None, :] < c0[:, None] + NA_WIN_C)
    dc = np.clip(cols[None, :] - cols[:, None] + NA_WIN_C - 1, 0, 2 * NA_WIN_C - 2)
    pick = (dc[None] == np.arange(2 * NA_WIN_C - 1)[:, None, None]).astype(np.float32)
    by_col = jnp.einsum("hrd,dqk->hrqk", rpb.astype(F32), jnp.asarray(pick), precision=lax.Precision.HIGHEST)
    by_col = jnp.where(col_mask[None, None], by_col * LOG2E, NEG_INF)
    tab = jnp.stack([by_col[:, v:v + NA_WIN_R] for v in range(NA_WIN_R)], axis=0)
    return jnp.transpose(tab, (0, 1, 3, 2, 4)).reshape(NA_WIN_R, NA_HEADS, GRID_W, NA_WIN_R * GRID_W)


def _na_kernel(rows, q_ref, kp_ref, kc_ref, kn_ref, vp_ref, vc_ref, vn_ref, bias_ref, o_ref, kbuf, vbuf):
    i = pl.program_id(1)
    kbuf[0:NA_TOK] = kp_ref[...]
    kbuf[NA_TOK:2 * NA_TOK] = kc_ref[...]
    kbuf[2 * NA_TOK:3 * NA_TOK] = kn_ref[...]
    vbuf[0:NA_TOK] = vp_ref[...]
    vbuf[NA_TOK:2 * NA_TOK] = vc_ref[...]
    vbuf[2 * NA_TOK:3 * NA_TOK] = vn_ref[...]
    win = NA_WIN_R * GRID_W
    for rl in range(NA_ROWS_PER_STEP):
        r = i * NA_ROWS_PER_STEP + rl
        r0 = jnp.clip(r - NA_WIN_R // 2, 0, rows - NA_WIN_R)
        variant = r0 - r + NA_WIN_R - 1
        off = pl.multiple_of((r0 - (i - 1) * NA_ROWS_PER_STEP) * GRID_W, GRID_W)
        kw = kbuf[pl.ds(off, win), :]
        vw = vbuf[pl.ds(off, win), :]
        q = q_ref[rl * GRID_W:(rl + 1) * GRID_W, :]
        for h in range(NA_HEADS):
            hs = slice(h * NA_DIM, (h + 1) * NA_DIM)
            s = _dot_nt(q[:, hs], kw[:, hs]) + bias_ref[variant, h]
            m = jnp.max(s, axis=1, keepdims=True)
            p = jnp.exp2(s - m)
            l = jnp.sum(p, axis=1, keepdims=True)
            o = _dot(p.astype(BF), vw[:, hs]) / l
            o_ref[rl * GRID_W:(rl + 1) * GRID_W, hs] = o.astype(BF)


def _na_attention(z, bias, batch, n_seq):
    T = z.shape[0]
    rows = n_seq // GRID_W
    nblk = rows // NA_ROWS_PER_STEP
    assert rows % NA_ROWS_PER_STEP == 0 and rows >= 2 * NA_ROWS_PER_STEP

    def cur(c):
        return pl.BlockSpec((NA_TOK, MIX), lambda b, i: (b * nblk + i, c))

    def prev(c):
        return pl.BlockSpec((NA_TOK, MIX), lambda b, i: (b * nblk + jnp.maximum(i - 1, 0), c))

    def nxt(c):
        return pl.BlockSpec((NA_TOK, MIX), lambda b, i: (b * nblk + jnp.minimum(i + 1, nblk - 1), c))

    ck, cv = Z_KA // MIX, Z_VA // MIX
    return pl.pallas_call(
        functools.partial(_na_kernel, rows),
        grid=(batch, nblk),
        in_specs=[cur(Z_QA // MIX), prev(ck), cur(ck), nxt(ck), prev(cv), cur(cv), nxt(cv),
                  pl.BlockSpec(bias.shape, lambda b, i: (0, 0, 0, 0))],
        out_specs=pl.BlockSpec((NA_TOK, MIX), lambda b, i: (b * nblk + i, 0)),
        out_shape=jax.ShapeDtypeStruct((T, MIX), BF),
        scratch_shapes=[pltpu.VMEM((3 * NA_TOK, MIX), BF), pltpu.VMEM((3 * NA_TOK, MIX), BF)],
        compiler_params=_params("parallel", "parallel"),
        name="na_attention",
    )(z, z, z, z, z, z, z, bias)


MLA_QK = 256
MLA_Q_SCALE = (MLA_NOPE + MLA_ROPE) ** -0.5 * LOG2E


def _mla_weights(w_q_up, w_kv_up):
    half = MLA_ROPE // 2
    wq = w_q_up.reshape(MLA_Q_RANK, MLA_HEADS, MLA_NOPE + MLA_ROPE)
    pe = wq[:, :, MLA_NOPE:]
    rot = jnp.concatenate([-pe[:, :, half:], pe[:, :, :half]], axis=-1)
    wq = jnp.concatenate([wq, rot], axis=-1).reshape(MLA_Q_RANK, MLA_HEADS * MLA_QK)
    wkv = w_kv_up.reshape(MLA_KV_RANK, MLA_HEADS, MLA_NOPE + MLA_V)
    wk = wkv[:, :, :MLA_NOPE].reshape(MLA_KV_RANK, MLA_HEADS * MLA_NOPE)
    wv = wkv[:, :, MLA_NOPE:].reshape(MLA_KV_RANK, MLA_HEADS * MLA_V)
    return wq.astype(BF), wk.astype(BF), wv.astype(BF)


def _rope_table(n):
    inv = 1.0 / (ROPE_BASE ** (jnp.arange(0, MLA_ROPE, 2, dtype=F32) / MLA_ROPE))
    ang = jnp.arange(n, dtype=F32)[:, None] * inv[None, :]
    c, s = jnp.cos(ang), jnp.sin(ang)
    return jnp.concatenate([c, c, s, s], axis=-1)


def _mla_prep_kernel(z_ref, cs_ref, gq_ref, gkv_ref, wq_ref, wk_ref, wv_ref, q_ref, k_ref, v_ref):
    z = z_ref[...].astype(F32)
    qn = _rms(z[:, :MLA_Q_RANK], gq_ref[...], RMS_EPS).astype(BF)
    kvn = _rms(z[:, MLA_Q_RANK:MLA_Q_RANK + MLA_KV_RANK], gkv_ref[...], RMS_EPS).astype(BF)
    q = _dot(qn, wq_ref[...])
    kn = _dot(kvn, wk_ref[...])
    v_ref[...] = _dot(kvn, wv_ref[...]).astype(BF)
    cs = cs_ref[...]
    keep = lax.broadcasted_iota(jnp.int32, cs.shape, 1) < MLA_ROPE

    def rope(g):
        t = g * cs
        return jnp.where(keep, t + pltpu.roll(t, MLA_ROPE, 1), 0.0)

    k_pe = rope(z[:, 640:768]).astype(BF)
    for h in range(MLA_HEADS):
        lo, mid, hi = h * MLA_QK, h * MLA_QK + MLA_NOPE, (h + 1) * MLA_QK
        q_ref[:, lo:mid] = (q[:, lo:mid] * MLA_Q_SCALE).astype(BF)
        q_ref[:, mid:hi] = (rope(q[:, mid:hi]) * MLA_Q_SCALE).astype(BF)
        k_ref[:, lo:mid] = kn[:, h * MLA_NOPE:(h + 1) * MLA_NOPE].astype(BF)
        k_ref[:, mid:hi] = k_pe


def _mla_prep(z, cs, gq, gkv, wq, wk, wv, n_seq):
    T = z.shape[0]
    tm = 512
    tpb = n_seq // tm

    def full(a):
        return pl.BlockSpec(a.shape, lambda i: (0, 0))

    return pl.pallas_call(
        _mla_prep_kernel,
        grid=(T // tm,),
        in_specs=[pl.BlockSpec((tm, Z_MLA_W), lambda i: (i, Z_MLA // Z_MLA_W)),
                  pl.BlockSpec((tm, 128), lambda i: (i % tpb, 0)),
                  full(gq), full(gkv), full(wq), full(wk), full(wv)],
        out_specs=[pl.BlockSpec((tm, MLA_HEADS * MLA_QK), lambda i: (i, 0)),
                   pl.BlockSpec((tm, MLA_HEADS * MLA_QK), lambda i: (i, 0)),
                   pl.BlockSpec((tm, MLA_HEADS * MLA_V), lambda i: (i, 0))],
        out_shape=[jax.ShapeDtypeStruct((T, MLA_HEADS * MLA_QK), BF),
                   jax.ShapeDtypeStruct((T, MLA_HEADS * MLA_QK), BF),
                   jax.ShapeDtypeStruct((T, MLA_HEADS * MLA_V), BF)],
        compiler_params=_params("parallel"),
        name="mla_prep",
    )(z, cs, gq, gkv, wq, wk, wv)


def _flash_init(m_ref, l_ref, acc_ref):
    m_ref[...] = jnp.full(m_ref.shape, NEG_INF, F32)
    l_ref[...] = jnp.zeros(l_ref.shape, F32)
    acc_ref[...] = jnp.zeros(acc_ref.shape, F32)


def _flash_step(s_t, v, m_ref, l_ref, acc_ref, row=None):
    m_prev = m_ref[...]
    m_tile = jnp.max(s_t, axis=0, keepdims=True)
    m_new = jnp.maximum(m_prev, m_tile if row is None else m_tile - row)
    alpha = jnp.exp2(m_prev - m_new)
    p_t = jnp.exp2(s_t - (m_new if row is None else m_new + row))
    l_ref[...] = alpha * l_ref[...] + jnp.sum(p_t, axis=0, keepdims=True)
    pv = lax.dot_general(v, p_t.astype(BF), (((0,), (0,)), ((), ())), preferred_element_type=F32)
    acc_ref[...] = alpha * acc_ref[...] + pv
    m_ref[...] = m_new


FLASH_TQ, FLASH_TK = 512, 1024


def _mla_flash_kernel(q_ref, k_ref, v_ref, o_ref, m_ref, l_ref, acc_ref):
    j = pl.program_id(2)

    @pl.when(j == 0)
    def _():
        _flash_init(m_ref, l_ref, acc_ref)

    for h in range(MLA_HEADS):
        qk = slice(h * MLA_QK, (h + 1) * MLA_QK)
        s_t = _dot_nt(k_ref[:, qk], q_ref[:, qk])
        _flash_step(s_t, v_ref[:, h * MLA_V:(h + 1) * MLA_V], m_ref.at[h], l_ref.at[h], acc_ref.at[h])

    @pl.when(j == pl.num_programs(2) - 1)
    def _():
        for h in range(MLA_HEADS):
            o_ref[:, h * MLA_V:(h + 1) * MLA_V] = (acc_ref[h] / l_ref[h]).T.astype(BF)


def _mla_flash(q, k, v, batch, n_seq):
    T = q.shape[0]
    tq, tk = FLASH_TQ, FLASH_TK
    nq, nk = n_seq // tq, n_seq // tk
    H = MLA_HEADS
    return pl.pallas_call(
        _mla_flash_kernel,
        grid=(batch, nq, nk),
        in_specs=[pl.BlockSpec((tq, H * MLA_QK), lambda b, i, j: (b * nq + i, 0)),
                  pl.BlockSpec((tk, H * MLA_QK), lambda b, i, j: (b * nk + j, 0)),
                  pl.BlockSpec((tk, H * MLA_V), lambda b, i, j: (b * nk + j, 0))],
        out_specs=pl.BlockSpec((tq, H * MLA_V), lambda b, i, j: (b * nq + i, 0)),
        out_shape=jax.ShapeDtypeStruct((T, H * MLA_V), BF),
        scratch_shapes=[pltpu.VMEM((H, 1, tq), F32), pltpu.VMEM((H, 1, tq), F32), pltpu.VMEM((H, MLA_V, tq), F32)],
        compiler_params=_params("parallel", "parallel", "arbitrary"),
        name="mla_flash",
    )(q, k, v)


DIFF_TQ, DIFF_TK = 512, 1024
DIFF_W = 2 * DIFF_DIM


def _diff_flash_kernel(par_ref, q_ref, k_ref, v_ref, d_ref, g_ref, o_ref, q1_ref, q2_ref, m_ref, l_ref, acc_ref):
    i, j = pl.program_id(1), pl.program_id(2)

    @pl.when(j == 0)
    def _():
        q = q_ref[...]
        first = lax.broadcasted_iota(jnp.int32, q.shape, 1) % DIFF_W < DIFF_DIM
        zero = jnp.zeros_like(q)
        q1_ref[...] = jnp.where(first, q, zero)
        q2_ref[...] = jnp.where(first, zero, q)
        _flash_init(m_ref, l_ref, acc_ref)

    def chains(h, score_fix, row):
        hs = slice(h * DIFF_W, (h + 1) * DIFF_W)
        k, v = k_ref[:, hs], v_ref[:, hs]
        for c, qm_ref in enumerate((q1_ref, q2_ref)):
            n = 2 * h + c
            _flash_step(score_fix(_dot_nt(k, qm_ref[:, hs])), v, m_ref.at[n], l_ref.at[n], acc_ref.at[n], row)

    q_lo, k_lo = i * DIFF_TQ, j * DIFF_TK
    below = q_lo >= k_lo + DIFF_TK
    above = k_lo >= q_lo + DIFF_TQ
    crossing = jnp.logical_not(jnp.logical_or(below, above))

    @pl.when(crossing)
    def _():
        dist = jnp.abs(d_ref[...] + (q_lo - k_lo).astype(F32))
        for h in range(DIFF_HEADS):
            bias = par_ref[h] * dist
            chains(h, lambda s_t: s_t - bias, None)

    @pl.when(jnp.logical_not(crossing))
    def _():
        sign = jnp.where(below, 1.0, -1.0).astype(F32)
        kl = lax.broadcasted_iota(jnp.int32, (DIFF_TK, 128), 0).astype(F32)
        qpos = (lax.broadcasted_iota(jnp.int32, (1, DIFF_TQ), 1) + (q_lo - k_lo)).astype(F32)
        for h in range(DIFF_HEADS):
            slope = sign * par_ref[h]
            col = jnp.concatenate([slope * kl] * (DIFF_TQ // 128), axis=1)
            chains(h, lambda s_t: s_t + col, slope * qpos)

    @pl.when(j == pl.num_programs(2) - 1)
    def _():
        lam, out_scale = par_ref[DIFF_HEADS], par_ref[DIFF_HEADS + 1]
        for h in range(DIFF_HEADS):
            n = 2 * h
            o_t = acc_ref[n] / l_ref[n] - lam * (acc_ref[n + 1] / l_ref[n + 1])
            o_ref[:, h * DIFF_W:(h + 1) * DIFF_W] = (_rms(o_t.T, g_ref[...], DIFF_EPS) * out_scale).astype(BF)


def _diff_flash(z, par, subln_g, batch, n_seq):
    T = z.shape[0]
    tq, tk = DIFF_TQ, DIFF_TK
    nq, nk = n_seq // tq, n_seq // tk
    w = DIFF_HEADS * DIFF_W
    chains = 2 * DIFF_HEADS
    delta = (np.arange(tq)[None, :] - np.arange(tk)[:, None]).astype(np.float32)
    return pl.pallas_call(
        _diff_flash_kernel,
        grid=(batch, nq, nk),
        in_specs=[pl.BlockSpec(memory_space=pltpu.SMEM),
                  pl.BlockSpec((tq, w), lambda b, i, j: (b * nq + i, Z_QC // w)),
                  pl.BlockSpec((tk, w), lambda b, i, j: (b * nk + j, Z_KC // w)),
                  pl.BlockSpec((tk, w), lambda b, i, j: (b * nk + j, Z_VC // w)),
                  pl.BlockSpec((tk, tq), lambda b, i, j: (0, 0)),
                  pl.BlockSpec((1, DIFF_W), lambda b, i, j: (0, 0))],
        out_specs=pl.BlockSpec((tq, w), lambda b, i, j: (b * nq + i, 0)),
        out_shape=jax.ShapeDtypeStruct((T, w), BF),
        scratch_shapes=[pltpu.VMEM((tq, w), BF), pltpu.VMEM((tq, w), BF),
                        pltpu.VMEM((chains, 1, tq), F32), pltpu.VMEM((chains, 1, tq), F32),
                        pltpu.VMEM((chains, DIFF_W, tq), F32)],
        compiler_params=_params("parallel", "parallel", "arbitrary"),
        name="diff_flash",
    )(par, z, z, z, jnp.asarray(delta), subln_g)


def _alibi_slopes(n):
    return 2.0 ** (-8.0 * np.arange(1, n + 1) / n)


def _swa_bias_table():
    rel = np.arange(3 * SWA_BLOCK)[None, :] - SWA_BLOCK - np.arange(SWA_BLOCK)[:, None]
    dist = np.abs(rel).astype(np.float64)
    tab = -_alibi_slopes(SWA_HEADS)[:, None, None] * dist[None] * LOG2E
    tab = np.where((dist <= SWA_WINDOW)[None], tab, -np.inf)
    return tab.astype(np.float32)


def _swa_kernel(sink_ref, q_ref, kvp_ref, kvc_ref, kvn_ref, bias_ref, o_ref):
    i = pl.program_id(1)
    last = pl.num_programs(1) - 1
    kv = jnp.concatenate([kvp_ref[...], kvc_ref[...], kvn_ref[...]], axis=0)
    col = lax.broadcasted_iota(jnp.int32, (SWA_BLOCK, 3 * SWA_BLOCK), 1)
    outside = ((col < SWA_BLOCK) & (i == 0)) | ((col >= 2 * SWA_BLOCK) & (i == last))
    group = SWA_HEADS // SWA_KV_HEADS
    for kvh in range(SWA_KV_HEADS):
        k = kv[:, kvh * SWA_DIM:(kvh + 1) * SWA_DIM]
        v = kv[:, (SWA_KV_HEADS + kvh) * SWA_DIM:(SWA_KV_HEADS + kvh + 1) * SWA_DIM]
        for g in range(group):
            hh = kvh * group + g
            hs = slice(hh * SWA_DIM, (hh + 1) * SWA_DIM)
            s = _dot_nt(q_ref[:, hs], k) + bias_ref[hh]
            s = jnp.where(outside, NEG_INF, s)
            sink = sink_ref[hh]
            m = jnp.maximum(jnp.max(s, axis=1, keepdims=True), sink)
            e = jnp.exp2(s - m)
            denom = jnp.sum(e, axis=1, keepdims=True) + jnp.exp2(sink - m)
            o_ref[:, hs] = (_dot(e.astype(BF), v) / denom).astype(BF)


def _swa_attention(z, sinks2, batch, n_seq):
    T = z.shape[0]
    nb = n_seq // SWA_BLOCK
    kvw = 2 * SWA_KV_HEADS * SWA_DIM
    ckv = Z_KVD // kvw
    bias = jnp.asarray(_swa_bias_table())
    return pl.pallas_call(
        _swa_kernel,
        grid=(batch, nb),
        in_specs=[pl.BlockSpec(memory_space=pltpu.SMEM),
                  pl.BlockSpec((SWA_BLOCK, MIX), lambda b, i: (b * nb + i, Z_QD // MIX)),
                  pl.BlockSpec((SWA_BLOCK, kvw), lambda b, i: (b * nb + jnp.maximum(i - 1, 0), ckv)),
                  pl.BlockSpec((SWA_BLOCK, kvw), lambda b, i: (b * nb + i, ckv)),
                  pl.BlockSpec((SWA_BLOCK, kvw), lambda b, i: (b * nb + jnp.minimum(i + 1, nb - 1), ckv)),
                  pl.BlockSpec(bias.shape, lambda b, i: (0, 0, 0))],
        out_specs=pl.BlockSpec((SWA_BLOCK, MIX), lambda b, i: (b * nb + i, 0)),
        out_shape=jax.ShapeDtypeStruct((T, MIX), BF),
        compiler_params=_params("parallel", "parallel"),
        name="swa_attention",
    )(sinks2, z, z, z, z, bias)


def _merge_kernel(oa_ref, ob_ref, oc_ref, od_ref, w_ref, g0_ref, g1_ref, g2_ref, g3_ref, out_ref):
    acc = None
    for b, (o_ref, g_ref) in enumerate(((oa_ref, g0_ref), (ob_ref, g1_ref), (oc_ref, g2_ref), (od_ref, g3_ref))):
        y = _sigmoid(g_ref[...].astype(F32)) * _dot(o_ref[...], w_ref[b])
        acc = y if acc is None else acc + y
    out_ref[...] = acc.astype(BF)


def _merge(outs, z, w_branch):
    T = z.shape[0]
    tm, tn = 512, 512
    o_spec = pl.BlockSpec((tm, MIX), lambda i, j: (i, 0))

    def gate(b):
        return pl.BlockSpec((tm, tn), lambda i, j: (i, (Z_GATE + b * D_MODEL) // tn + j))

    return pl.pallas_call(
        _merge_kernel,
        grid=(T // tm, D_MODEL // tn),
        in_specs=[o_spec, o_spec, o_spec, o_spec,
                  pl.BlockSpec((N_BRANCH, MIX, tn), lambda i, j: (0, 0, j)),
                  gate(0), gate(1), gate(2), gate(3)],
        out_specs=pl.BlockSpec((tm, tn), lambda i, j: (i, j)),
        out_shape=jax.ShapeDtypeStruct((T, D_MODEL), BF),
        compiler_params=_params("parallel", "parallel"),
        name="merge",
    )(*outs, w_branch, z, z, z, z)


PACK_ROWS = 8
PACK_LANES = 128
_HALF = PACK_ROWS * PACK_LANES


def _f32_bits(x):
    return lax.bitcast_convert_type(x, jnp.uint32)


def _pack_rows(ref, x, n):
    lo = _f32_bits(x[:, :_HALF].astype(BF).astype(F32)) >> 16
    hi = _f32_bits(x[:, _HALF:].astype(BF).astype(F32)) & jnp.uint32(0xFFFF0000)
    w = hi | lo
    for s in range(PACK_ROWS):
        ref[pl.ds(s, n, stride=PACK_ROWS), :] = w[:, s * PACK_LANES:(s + 1) * PACK_LANES]


def _unpack_rows(ref, n):
    lo, hi = [], []
    for s in range(PACK_ROWS):
        w = ref[pl.ds(s, n, stride=PACK_ROWS), :]
        lo.append(lax.bitcast_convert_type(w << 16, F32))
        hi.append(lax.bitcast_convert_type(w & jnp.uint32(0xFFFF0000), F32))
    return jnp.concatenate(lo + hi, axis=1)


def _out_proj_kernel(m_ref, w_ref, x_ref, g1_ref, n2_ref, sh_ref, sc_ref, wr_ref, x1_ref, h2_ref, hp_ref, lg_ref):
    x1 = x_ref[...] + g1_ref[0] * _dot(m_ref[...], w_ref[...])
    x1_ref[...] = x1
    h2 = _rms(x1, n2_ref[...], RMS_EPS) * (1.0 + sc_ref[0]) + sh_ref[0]
    h2_ref[...] = h2.astype(BF)
    _pack_rows(hp_ref, h2, h2.shape[0])
    lg_ref[...] = _dot_nt(wr_ref[...], h2, precision=lax.Precision.HIGHEST)


def _out_proj(merged, w_o, x, g1, n2, sh2, sc2, w_router_t, n_seq):
    T, D = x.shape
    tm = 256
    tpb = n_seq // tm
    row = lambda: pl.BlockSpec((tm, D), lambda i: (i, 0))
    per_batch = lambda: pl.BlockSpec((1, 1, D), lambda i: (i // tpb, 0, 0))
    return pl.pallas_call(
        _out_proj_kernel,
        grid=(T // tm,),
        in_specs=[row(), pl.BlockSpec((D, D), lambda i: (0, 0)), row(), per_batch(),
                  pl.BlockSpec((1, D), lambda i: (0, 0)), per_batch(), per_batch(),
                  pl.BlockSpec((N_EXPERTS, D), lambda i: (0, 0))],
        out_specs=[row(), row(), pl.BlockSpec((tm * PACK_ROWS, PACK_LANES), lambda i: (i, 0)),
                   pl.BlockSpec((N_EXPERTS, tm), lambda i: (0, i))],
        out_shape=[jax.ShapeDtypeStruct((T, D), F32), jax.ShapeDtypeStruct((T, D), BF),
                   jax.ShapeDtypeStruct((T * PACK_ROWS, PACK_LANES), jnp.uint32),
                   jax.ShapeDtypeStruct((N_EXPERTS, T), F32)],
        compiler_params=_params("parallel"),
        name="out_proj",
    )(merged, w_o, x, g1, n2, sh2, sc2, w_router_t)


def _route_kernel(lg_ref, b_ref, idx_ref, w_ref):
    s = _sigmoid(lg_ref[...])
    sel = s + b_ref[...]
    shape = sel.shape
    member = lax.broadcasted_iota(jnp.int32, shape, 1)
    eid = lax.broadcasted_iota(jnp.int32, shape, 0) * GROUP_SIZE + member

    def first_max(x, ids, axes, sentinel):
        m = x
        for a in axes:
            m = jnp.max(m, axis=a, keepdims=True)
        f = jnp.where(x == m, ids, sentinel)
        for a in axes:
            f = jnp.min(f, axis=a, keepdims=True)
        return f

    top1 = member == first_max(sel, member, (1,), GROUP_SIZE)
    m1 = jnp.max(sel, axis=1, keepdims=True)
    m2 = jnp.max(jnp.where(top1, NEG_INF, sel), axis=1, keepdims=True)
    score = m1 + m2
    gid = lax.broadcasted_iota(jnp.int32, score.shape, 0)
    chosen = jnp.zeros(score.shape, jnp.int32)
    for _ in range(TOPK_GROUPS):
        pick = gid == first_max(score, gid, (0,), N_GROUPS)
        chosen = jnp.where(pick, 1, chosen)
        score = jnp.where(pick, NEG_INF, score)
    cur = jnp.where(jnp.broadcast_to(chosen, shape) > 0, sel, NEG_INF)
    picked_w = []
    for k in range(TOP_K):
        f = first_max(cur, eid, (1, 0), N_EXPERTS)
        pick = eid == f
        wk = jnp.sum(jnp.sum(jnp.where(pick, s, 0.0), axis=1, keepdims=True), axis=0, keepdims=True)
        idx_ref[k:k + 1, :] = f[0]
        picked_w.append(wk[0])
        cur = jnp.where(pick, NEG_INF, cur)
    total = picked_w[0]
    for wk in picked_w[1:]:
        total = total + wk
    for k, wk in enumerate(picked_w):
        w_ref[k:k + 1, :] = wk / total * ROUTED_SCALE


def _route(logits_t, b_router):
    T = logits_t.shape[1]
    tt = 512
    lg = logits_t.reshape(N_GROUPS, GROUP_SIZE, T)
    return pl.pallas_call(
        _route_kernel,
        grid=(T // tt,),
        in_specs=[pl.BlockSpec((N_GROUPS, GROUP_SIZE, tt), lambda i: (0, 0, i)),
                  pl.BlockSpec((N_GROUPS, GROUP_SIZE, 1), lambda i: (0, 0, 0))],
        out_specs=[pl.BlockSpec((TOP_K, tt), lambda i: (0, i)), pl.BlockSpec((TOP_K, tt), lambda i: (0, i))],
        out_shape=[jax.ShapeDtypeStruct((TOP_K, T), jnp.int32), jax.ShapeDtypeStruct((TOP_K, T), F32)],
        compiler_params=_params("parallel"),
        name="route",
    )(lg, b_router.astype(F32).reshape(N_GROUPS, GROUP_SIZE, 1))


def _swiglu_tile(x, wg, wu, wd):
    a = _silu(_dot(x, wg)) * _dot(x, wu)
    return _dot(a.astype(BF), wd)


BLOCK_PACKED_ROWS = MOE_BLOCK * PACK_ROWS
DMA_ISSUE_UNROLL = 8


def _token_rows(ref, t):
    return ref.at[pl.ds(pl.multiple_of(t * PACK_ROWS, PACK_ROWS), PACK_ROWS)]


def _experts_kernel(n_blocks, blk_e_ref, src0_ref, src_next_ref, dst_ref, h_hbm, wg_ref, wu_ref, wd_ref, y_hbm,
                    xbuf, ybuf, gsem, ssem):
    i = pl.program_id(0)
    slot = lax.rem(i, 2)
    other = 1 - slot

    def start_gather(idx_ref, s):
        for r in range(MOE_BLOCK):
            pltpu.make_async_copy(_token_rows(h_hbm, idx_ref[0, 0, r]), _token_rows(xbuf.at[s], r), gsem.at[s]).start()

    def wait_gather(s):
        pltpu.make_async_copy(h_hbm.at[pl.ds(0, BLOCK_PACKED_ROWS)], xbuf.at[s], gsem.at[s]).wait()

    def wait_scatter(s):
        pltpu.make_async_copy(ybuf.at[s], y_hbm.at[pl.ds(0, BLOCK_PACKED_ROWS)], ssem.at[s]).wait()

    @pl.when(i == 0)
    def _():
        start_gather(src0_ref, 0)

    start_gather(src_next_ref, other)
    wait_gather(slot)
    x = _unpack_rows(xbuf.at[slot], MOE_BLOCK).astype(BF)
    y = _swiglu_tile(x, wg_ref[0], wu_ref[0], wd_ref[0])

    @pl.when(i >= 2)
    def _():
        wait_scatter(slot)

    _pack_rows(ybuf.at[slot], y, MOE_BLOCK)
    for r in range(MOE_BLOCK):
        pltpu.make_async_copy(_token_rows(ybuf.at[slot], r), _token_rows(y_hbm, dst_ref[0, 0, r]), ssem.at[slot]).start()

    @pl.when(i == n_blocks - 1)
    def _():
        wait_gather(other)
        wait_scatter(slot)
        if n_blocks > 1:
            wait_scatter(other)


def _experts_fused(h_rows, src_tok, dst_slot, blk_e, wg, wu, wd):
    n_blocks = blk_e.shape[0]
    P = n_blocks * MOE_BLOCK
    D = wg.shape[1]
    idx_block = lambda f: pl.BlockSpec((1, 1, MOE_BLOCK), f, memory_space=pltpu.SMEM)
    buf = lambda: pltpu.VMEM((2, BLOCK_PACKED_ROWS, PACK_LANES), jnp.uint32)
    return pl.pallas_call(
        functools.partial(_experts_kernel, n_blocks),
        grid_spec=pltpu.PrefetchScalarGridSpec(
            num_scalar_prefetch=1,
            grid=(n_blocks,),
            in_specs=[idx_block(lambda i, e: (0, 0, 0)),
                      idx_block(lambda i, e: (jnp.minimum(i + 1, n_blocks - 1), 0, 0)),
                      idx_block(lambda i, e: (i, 0, 0)),
                      pl.BlockSpec(memory_space=pl.ANY),
                      pl.BlockSpec((1, D, EXPERT_DIM), lambda i, e: (e[i], 0, 0)),
                      pl.BlockSpec((1, D, EXPERT_DIM), lambda i, e: (e[i], 0, 0)),
                      pl.BlockSpec((1, EXPERT_DIM, D), lambda i, e: (e[i], 0, 0))],
            out_specs=pl.BlockSpec(memory_space=pl.ANY),
            scratch_shapes=[buf(), buf(), pltpu.SemaphoreType.DMA((2,)), pltpu.SemaphoreType.DMA((2,))]),
        out_shape=jax.ShapeDtypeStruct((P * PACK_ROWS, PACK_LANES), jnp.uint32),
        compiler_params=_params("arbitrary"),
        name="experts",
    )(blk_e, *([src_tok.reshape(n_blocks, 1, MOE_BLOCK)] * 2), dst_slot.reshape(n_blocks, 1, MOE_BLOCK),
      h_rows, wg, wu, wd)


def _combine_kernel(final, x1_ref, h2_ref, w_ref, g2_ref, wg_ref, wu_ref, wd_ref, fg_ref, *rest):
    y_refs, out_ref = rest[:TOP_K], rest[TOP_K]
    w = w_ref[...]
    acc = _swiglu_tile(h2_ref[...], wg_ref[...], wu_ref[...], wd_ref[...])
    for k, y_ref in enumerate(y_refs):
        acc = acc + w[:, k:k + 1] * _unpack_rows(y_ref, acc.shape[0])
    x2 = x1_ref[...] + g2_ref[0] * acc
    out_ref[...] = _rms(x2, fg_ref[...], RMS_EPS) if final else x2


def _combine(x1, h2, w_tok, g2, wsg, wsu, wsd, final_g, ys, n_seq, final):
    T, D = x1.shape
    tm = 256
    tpb = n_seq // tm
    nt = T // tm
    row = lambda: pl.BlockSpec((tm, D), lambda i: (i, 0))
    full = lambda a: pl.BlockSpec(a.shape, lambda i: (0, 0))

    def slot(k):
        return pl.BlockSpec((tm * PACK_ROWS, PACK_LANES), lambda i: (k * nt + i, 0))

    return pl.pallas_call(
        functools.partial(_combine_kernel, final),
        grid=(nt,),
        in_specs=[row(), row(), pl.BlockSpec((tm, TOP_K), lambda i: (i, 0)),
                  pl.BlockSpec((1, 1, D), lambda i: (i // tpb, 0, 0)),
                  full(wsg), full(wsu), full(wsd), full(final_g)] + [slot(k) for k in range(TOP_K)],
        out_specs=row(),
        out_shape=jax.ShapeDtypeStruct((T, D), F32),
        compiler_params=_params("parallel"),
        name="combine",
    )(x1, h2, w_tok, g2, wsg, wsu, wsd, final_g, *([ys] * TOP_K))


def _dispatch_plan(idx_t):
    K, T = idx_t.shape
    TK = K * T
    n_blocks = TK // MOE_BLOCK + N_EXPERTS
    i32 = jnp.int32
    e_flat = idx_t.reshape(TK)
    order = jnp.argsort(e_flat).astype(i32)
    experts = jnp.arange(N_EXPERTS, dtype=i32)
    counts = jnp.sum((e_flat[None, :] == experts[:, None]).astype(i32), axis=1)
    start = jnp.cumsum(counts) - counts
    padded = (counts + MOE_BLOCK - 1) // MOE_BLOCK * MOE_BLOCK
    p_end = jnp.cumsum(padded)
    p_start = p_end - padded
    pads = padded - counts
    pad_before = jnp.cumsum(pads) - pads
    blk_first = jnp.arange(n_blocks, dtype=i32) * MOE_BLOCK
    blk_e = jnp.minimum(jnp.sum((p_end[None, :] <= blk_first[:, None]).astype(i32), axis=1), N_EXPERTS - 1)
    j = (blk_first - p_start[blk_e])[:, None] + jnp.arange(MOE_BLOCK, dtype=i32)[None, :]
    cnt = counts[blk_e][:, None]
    valid = j < cnt
    flat = order[jnp.clip(start[blk_e][:, None] + j, 0, TK - 1)]
    src_tok = jnp.where(valid, flat % T, 0).astype(i32)
    dst_slot = jnp.where(valid, flat, TK + pad_before[blk_e][:, None] + j - cnt).astype(i32)
    return src_tok, dst_slot, blk_e.astype(i32)


def _layer_weights(l, p):
    f = lambda a: a.astype(BF)
    w_in = _in_proj_weight(p["w_in"][l])
    wq, wk, wv = _mla_weights(p["mla_w_q_up"][l], p["mla_w_kv_up"][l])
    lam_init = 0.8 - 0.6 * math.exp(-0.3 * l)
    lam = (jnp.exp(jnp.sum(p["diff_lambda_q1"][l] * p["diff_lambda_k1"][l]))
           - jnp.exp(jnp.sum(p["diff_lambda_q2"][l] * p["diff_lambda_k2"][l])) + lam_init)
    diff_par = jnp.concatenate([jnp.asarray(_alibi_slopes(DIFF_HEADS) * LOG2E, F32),
                                jnp.stack([lam, jnp.asarray(1.0 - lam_init, F32)]).astype(F32),
                                jnp.zeros((2,), F32)])
    return dict(
        w_in=w_in, wq=wq, wk=wk, wv=wv,
        gq=p["mla_q_norm_g"][l][None, :], gkv=p["mla_kv_norm_g"][l][None, :],
        na_bias=_na_bias_table(p["na_rpb"][l]),
        diff_par=diff_par, subln_g=p["diff_subln_g"][l][None, :],
        sinks2=p["swa_sinks"][l].astype(F32) * LOG2E,
        w_branch=f(p["w_branch"][l]).reshape(N_BRANCH, MIX, D_MODEL), w_o=f(p["w_o"][l]),
        norm1_g=p["norm1_g"][l][None, :], norm2_g=p["norm2_g"][l][None, :],
        w_router_t=p["w_router"][l].T, b_router=p["b_router"][l],
        weg=f(p["w_exp_gate"][l]), weu=f(p["w_exp_up"][l]), wed=f(p["w_exp_down"][l]),
        wsg=f(p["w_sh_gate"][l]), wsu=f(p["w_sh_up"][l]), wsd=f(p["w_sh_down"][l]),
    )


def _trunk(x, mods, weights, final_g):
    B, N, D = x.shape
    T = B * N
    x = x.reshape(T, D)
    cs = _rope_table(N)
    n_layers = len(weights)
    for l, w in enumerate(weights):
        sh1, sc1, g1, sh2, sc2, g2 = [m[:, None, :] for m in jnp.split(mods[l], ADA_CHUNKS, axis=-1)]
        z = _in_proj(x, w["norm1_g"], sh1, sc1, w["w_in"], N)
        o_a = _na_attention(z, w["na_bias"], B, N)
        q, k, v = _mla_prep(z, cs, w["gq"], w["gkv"], w["wq"], w["wk"], w["wv"], N)
        o_b = _mla_flash(q, k, v, B, N)
        o_c = _diff_flash(z, w["diff_par"], w["subln_g"], B, N)
        o_d = _swa_attention(z, w["sinks2"], B, N)
        merged = _merge((o_a, o_b, o_c, o_d), z, w["w_branch"])
        x1, h2, h2_rows, logits_t = _out_proj(merged, w["w_o"], x, g1, w["norm2_g"], sh2, sc2, w["w_router_t"], N)
        idx_t, w_t = _route(logits_t, w["b_router"])
        src_tok, dst_slot, blk_e = _dispatch_plan(idx_t)
        ys_tok = _experts_fused(h2_rows, src_tok, dst_slot, blk_e, w["weg"], w["weu"], w["wed"])
        x = _combine(x1, h2, w_t.T, g2, w["wsg"], w["wsu"], w["wsd"], final_g[None, :], ys_tok, N,
                     final=(l == n_layers - 1))
    return x.reshape(B, N, D)


def kernel(x_prompt, x_sample, c_prompt, c_sample, norm1_g, w_ada, b_ada, w_in, na_rpb, mla_q_norm_g, mla_w_q_up, mla_kv_norm_g, mla_w_kv_up, diff_lambda_q1, diff_lambda_k1, diff_lambda_q2, diff_lambda_k2, diff_subln_g, swa_sinks, w_branch, w_o, norm2_g, w_router, b_router, w_exp_gate, w_exp_up, w_exp_down, w_sh_gate, w_sh_up, w_sh_down, final_g):
    p = dict(norm1_g=norm1_g, w_in=w_in, na_rpb=na_rpb, mla_q_norm_g=mla_q_norm_g, mla_w_q_up=mla_w_q_up,
             mla_kv_norm_g=mla_kv_norm_g, mla_w_kv_up=mla_w_kv_up, diff_lambda_q1=diff_lambda_q1,
             diff_lambda_k1=diff_lambda_k1, diff_lambda_q2=diff_lambda_q2, diff_lambda_k2=diff_lambda_k2,
             diff_subln_g=diff_subln_g, swa_sinks=swa_sinks, w_branch=w_branch, w_o=w_o, norm2_g=norm2_g,
             w_router=w_router, b_router=b_router, w_exp_gate=w_exp_gate, w_exp_up=w_exp_up,
             w_exp_down=w_exp_down, w_sh_gate=w_sh_gate, w_sh_up=w_sh_up, w_sh_down=w_sh_down)
    n_layers = w_in.shape[0]
    bp, bs = c_prompt.shape[0], c_sample.shape[0]
    assert bp + bs <= 8
    c_all = jnp.concatenate([c_prompt, c_sample, jnp.zeros((8 - bp - bs, c_prompt.shape[1]), F32)], axis=0)
    mods = _ada(c_all, w_ada, b_ada)
    weights = [_layer_weights(l, p) for l in range(n_layers)]
    y_prompt = _trunk(x_prompt, mods[:, :bp], weights, final_g)
    y_sample = _trunk(x_sample, mods[:, bp:bp + bs], weights, final_g)
    return (y_prompt, y_sample)
```
